```python
import jax, jax.numpy as jnp
from jax import lax
import numpy as np

D_MODEL = 4096
BATCH = 4
SEQ = 4096
DEPTH = 1
DEC_BATCH = 2
DEC_SEQ = 8192
PAST_LEN = 128

HEAD_DIM = 128
MIX_W = D_MODEL
RET_HEADS = (MIX_W // 2) // HEAD_DIM
ATT_HEADS = (MIX_W // 2) // HEAD_DIM
ATT_KV_HEADS = ATT_HEADS // 4
RET_W = RET_HEADS * HEAD_DIM
ATT_Q_W = ATT_HEADS * HEAD_DIM
ATT_KV_W = ATT_KV_HEADS * HEAD_DIM
IN_W = 4 * RET_W + ATT_Q_W + 2 * ATT_KV_W
RET_CHUNK = 128
WINDOW = 128
BLOCK = 128
N_BUCKETS = 32
MAX_DISTANCE = 128
MEM_TOKENS = 256
MEM_HEADS = 4
MEM_W = MEM_HEADS * HEAD_DIM
D_FF = 4 * D_MODEL
ROPE_BASE = 10000.0
RMS_EPS = 1e-6
NEG_INF = -1e30

kernel_name = "hymba_retention_swa_t5bias_encoder"


def rmsnorm(x, g):
    xf = x.astype(jnp.float32)
    var = jnp.mean(xf * xf, axis=-1, keepdims=True)
    return (xf * lax.rsqrt(var + RMS_EPS) * g.astype(jnp.float32)).astype(x.dtype)


def head_rmsnorm(y):
    yf = y.astype(jnp.float32)
    var = jnp.mean(yf * yf, axis=-1, keepdims=True)
    return (yf * lax.rsqrt(var + RMS_EPS)).astype(y.dtype)


def rope(x, pos):
    half = x.shape[-1] // 2
    inv = ROPE_BASE ** (-jnp.arange(half, dtype=jnp.float32) / half)
    ang = pos[:, None] * inv[None, :]
    cos = jnp.cos(ang)[None, :, None, :]
    sin = jnp.sin(ang)[None, :, None, :]
    xf = x.astype(jnp.float32)
    x1, x2 = xf[..., :half], xf[..., half:]
    return jnp.concatenate([x1 * cos - x2 * sin, x1 * sin + x2 * cos], axis=-1).astype(x.dtype)


def chunk_retention(q, k, v, log_g, strict):
    b, s, h, d = q.shape
    dt = q.dtype
    nc = s // RET_CHUNK
    qc = q.reshape(b, nc, RET_CHUNK, h, d)
    kc = k.reshape(b, nc, RET_CHUNK, h, d)
    vc = v.reshape(b, nc, RET_CHUNK, h, d)
    pos = np.arange(RET_CHUNK)
    diff = pos[:, None] - pos[None, :]
    mask = (diff > 0) if strict else (diff >= 0)
    decay = jnp.where(mask[None], jnp.exp(log_g[:, None, None] * np.maximum(diff, 0).astype(np.float32)), 0.0).astype(dt)
    att = jnp.einsum('bcihd,bcjhd->bchij', qc, kc) * decay[None, None]
    y_inner = jnp.einsum('bchij,bcjhd->bcihd', att, vc)
    posf = pos.astype(np.float32)
    zeta = jnp.exp(log_g[:, None] * (RET_CHUNK - 1 - posf)[None, :]).astype(dt)
    xi = jnp.exp(log_g[:, None] * (posf + 1.0)[None, :]).astype(dt)
    chunk_decay = jnp.exp(log_g * RET_CHUNK).astype(dt)
    kv = jnp.einsum('bcjhd,hj,bcjhe->bchde', kc, zeta, vc)

    def step(state, kv_c):
        return state * chunk_decay[None, :, None, None] + kv_c, state

    _, r_prev = lax.scan(step, jnp.zeros((b, h, d, d), dt), jnp.moveaxis(kv, 1, 0))
    r_prev = jnp.moveaxis(r_prev, 0, 1)
    y_cross = jnp.einsum('bcihd,hi,bchde->bcihe', qc, xi, r_prev)
    return (y_inner + y_cross).reshape(b, s, h, d)


def t5_buckets(rel):
    nb = N_BUCKETS // 2
    max_exact = nb // 2
    base = np.where(rel > 0, nb, 0)
    n = np.abs(rel)
    large = max_exact + (np.log(np.maximum(n, 1) / max_exact) / np.log(MAX_DISTANCE / max_exact) * (nb - max_exact)).astype(np.int32)
    large = np.minimum(large, nb - 1)
    return (base + np.where(n < max_exact, n, large)).astype(np.int32)


def windowed_gqa(q, k, v, sink, rel_bias):
    b, s, H, d = q.shape
    KV = k.shape[2]
    G = H // KV
    nb = s // BLOCK
    qb = q.reshape(b, nb, BLOCK, KV, G, d)
    pad = ((0, 0), (BLOCK, BLOCK), (0, 0), (0, 0))
    kp = jnp.pad(k, pad).reshape(b, nb + 2, BLOCK, KV, d)
    vp = jnp.pad(v, pad).reshape(b, nb + 2, BLOCK, KV, d)
    kband = jnp.concatenate([kp[:, :-2], kp[:, 1:-1], kp[:, 2:]], axis=2)
    vband = jnp.concatenate([vp[:, :-2], vp[:, 1:-1], vp[:, 2:]], axis=2)
    qi = np.arange(BLOCK)[:, None]
    kj = np.arange(3 * BLOCK)[None, :] - BLOCK
    rel = kj - qi
    in_window = np.abs(rel) <= WINDOW
    bias = rel_bias.astype(jnp.float32)[t5_buckets(rel)]
    bias = jnp.transpose(bias, (2, 0, 1)).reshape(KV, G, BLOCK, 3 * BLOCK)
    key_pos = jnp.arange(nb)[:, None] * BLOCK + kj
    valid = (key_pos >= 0) & (key_pos < s)
    mask = jnp.asarray(in_window)[None] & valid[:, None, :]
    scores = jnp.einsum('bnqkgd,bnjkd->bnkgqj', qb, kband).astype(jnp.float32) * (HEAD_DIM ** -0.5)
    scores = jnp.where(mask[None, :, None, None], scores + bias[None, None], NEG_INF)
    sink_f = sink.astype(jnp.float32).reshape(KV, G)[None, None, :, :, None]
    m = jnp.maximum(jnp.max(scores, axis=-1), sink_f)
    p = jnp.exp(scores - m[..., None])
    denom = jnp.sum(p, axis=-1) + jnp.exp(sink_f - m)
    p = (p / denom[..., None]).astype(v.dtype)
    out = jnp.einsum('bnkgqj,bnjkd->bnqkgd', p, vband)
    return out.reshape(b, s, H * d)


def parallel_mixer(h, w_in, dec_f, dec_b, sink, rel_bias, w_out):
    b, s, _ = h.shape
    proj = h @ w_in
    splits = [RET_W, 2 * RET_W, 3 * RET_W, 4 * RET_W, 4 * RET_W + ATT_Q_W, 4 * RET_W + ATT_Q_W + ATT_KV_W]
    q_r, k_r, v_r, g_r, q_a, k_a, v_a = jnp.split(proj, splits, axis=-1)
    pos = jnp.arange(s, dtype=jnp.float32)
    q_r = rope(q_r.reshape(b, s, RET_HEADS, HEAD_DIM), pos)
    k_r = rope(k_r.reshape(b, s, RET_HEADS, HEAD_DIM), pos) * (HEAD_DIM ** -0.5)
    v_r = v_r.reshape(b, s, RET_HEADS, HEAD_DIM)
    log_gf = -jnp.exp(dec_f.astype(jnp.float32))
    log_gb = -jnp.exp(dec_b.astype(jnp.float32))
    y_f = chunk_retention(q_r, k_r, v_r, log_gf, strict=False)
    y_b = jnp.flip(chunk_retention(jnp.flip(q_r, 1), jnp.flip(k_r, 1), jnp.flip(v_r, 1), log_gb, strict=True), 1)
    y_ret = head_rmsnorm(y_f + y_b) * jax.nn.silu(g_r.reshape(b, s, RET_HEADS, HEAD_DIM))
    y_ret = y_ret.reshape(b, s, RET_W)
    y_att = windowed_gqa(q_a.reshape(b, s, ATT_HEADS, HEAD_DIM),
                         k_a.reshape(b, s, ATT_KV_HEADS, HEAD_DIM),
                         v_a.reshape(b, s, ATT_KV_HEADS, HEAD_DIM), sink, rel_bias)
    return jnp.concatenate([y_ret, y_att], axis=-1) @ w_out


def memory_cross_attention(h, m, w_cq, w_ckv, w_co):
    b, s, _ = h.shape
    M = m.shape[1]
    q = (h @ w_cq).reshape(b, s, MEM_HEADS, HEAD_DIM)
    k, v = jnp.split(m @ w_ckv, 2, axis=-1)
    k = k.reshape(b, M, MEM_HEADS, HEAD_DIM)
    v = v.reshape(b, M, MEM_HEADS, HEAD_DIM)
    sc = jnp.einsum('bshd,bmhd->bhsm', q, k).astype(jnp.float32) * (HEAD_DIM ** -0.5)
    p = jax.nn.softmax(sc, axis=-1).astype(v.dtype)
    o = jnp.einsum('bhsm,bmhd->bshd', p, v).reshape(b, s, MEM_W)
    return o @ w_co


def squared_relu_mlp(h, w1, w2):
    a = jax.nn.relu(h @ w1)
    return (a * a) @ w2


def encode(x, mem, norm_mix, w_in, ret_decay_f, ret_decay_b, attn_sink, rel_bias, w_out,
           norm_cross, norm_mem, w_cq, w_ckv, w_co, norm_mlp, w_mlp_in, w_mlp_out, norm_final):
    for l in range(DEPTH):
        x = x + parallel_mixer(rmsnorm(x, norm_mix[l]), w_in[l], ret_decay_f[l], ret_decay_b[l],
                               attn_sink[l], rel_bias, w_out[l])
        x = x + memory_cross_attention(rmsnorm(x, norm_cross[l]), rmsnorm(mem, norm_mem[l]),
                                       w_cq[l], w_ckv[l], w_co[l])
        x = x + squared_relu_mlp(rmsnorm(x, norm_mlp[l]), w_mlp_in[l], w_mlp_out[l])
    return rmsnorm(x, norm_final)


def setup_inputs(seed: int = 0) -> dict:
    key = jax.random.key(seed)
    ks = jax.random.split(key, 20)
    f32 = jnp.float32

    def nrm(k, shape, fan_in):
        return jax.random.normal(k, shape, f32) * (fan_in ** -0.5)

    def gain(k, shape):
        return 1.0 + 0.01 * jax.random.normal(k, shape, f32)

    base_decay = jnp.log(-jnp.log(1.0 - 2.0 ** (-5.0 - jnp.arange(RET_HEADS, dtype=f32))))
    return {
        "x_prompt": jax.random.normal(ks[0], (BATCH, SEQ, D_MODEL), f32),
        "x_sample": jax.random.normal(ks[1], (DEC_BATCH, DEC_SEQ, D_MODEL), f32),
        "mem_prompt": jax.random.normal(ks[2], (BATCH, MEM_TOKENS, D_MODEL), f32),
        "mem_sample": jax.random.normal(ks[3], (DEC_BATCH, MEM_TOKENS, D_MODEL), f32),
        "norm_mix": gain(ks[4], (DEPTH, D_MODEL)),
        "w_in": nrm(ks[5], (DEPTH, D_MODEL, IN_W), D_MODEL),
        "ret_decay_f": base_decay[None] + 0.01 * jax.random.normal(ks[6], (DEPTH, RET_HEADS), f32),
        "ret_decay_b": base_decay[None] + 0.01 * jax.random.normal(ks[7], (DEPTH, RET_HEADS), f32),
        "attn_sink": 0.5 * jax.random.normal(ks[8], (DEPTH, ATT_HEADS), f32),
        "rel_bias": 0.5 * jax.random.normal(ks[9], (N_BUCKETS, ATT_HEADS), f32),
        "w_out": nrm(ks[10], (DEPTH, MIX_W, D_MODEL), MIX_W),
        "norm_cross": gain(ks[11], (DEPTH, D_MODEL)),
        "norm_mem": gain(ks[12], (DEPTH, D_MODEL)),
        "w_cq": nrm(ks[13], (DEPTH, D_MODEL, MEM_W), D_MODEL),
        "w_ckv": nrm(ks[14], (DEPTH, D_MODEL, 2 * MEM_W), D_MODEL),
        "w_co": nrm(ks[15], (DEPTH, MEM_W, D_MODEL), MEM_W),
        "norm_mlp": gain(ks[16], (DEPTH, D_MODEL)),
        "w_mlp_in": nrm(ks[17], (DEPTH, D_MODEL, D_FF), D_MODEL),
        "w_mlp_out": nrm(ks[18], (DEPTH, D_FF, D_MODEL), D_FF),
        "norm_final": gain(ks[19], (D_MODEL,)),
    }


def reference(x_prompt, x_sample, mem_prompt, mem_sample, norm_mix, w_in, ret_decay_f, ret_decay_b,
              attn_sink, rel_bias, w_out, norm_cross, norm_mem, w_cq, w_ckv, w_co, norm_mlp,
              w_mlp_in, w_mlp_out, norm_final):
    y_prompt = encode(x_prompt, mem_prompt, norm_mix, w_in, ret_decay_f, ret_decay_b, attn_sink, rel_bias,
                      w_out, norm_cross, norm_mem, w_cq, w_ckv, w_co, norm_mlp, w_mlp_in, w_mlp_out, norm_final)
    y_sample = encode(x_sample, mem_sample, norm_mix, w_in, ret_decay_f, ret_decay_b, attn_sink, rel_bias,
                      w_out, norm_cross, norm_mem, w_cq, w_ckv, w_co, norm_mlp, w_mlp_in, w_mlp_out, norm_final)
    return (y_prompt, y_sample)
```

```python
import functools
import math

import numpy as np
import jax
import jax.numpy as jnp
from jax import lax
from jax.experimental import pallas as pl
from jax.experimental.pallas import tpu as pltpu

HEAD_DIM = 128
RET_CHUNK = 128
WINDOW = 128
BLOCK = 128
N_BUCKETS = 32
MAX_DISTANCE = 128
MEM_HEADS = 4
GQA_GROUP = 4
ROPE_BASE = 10000.0
RMS_EPS = 1e-6
NEG_INF = -1e30

VMEM_LIMIT_BYTES = 56 * 1024 * 1024

BF16 = jnp.bfloat16
F32 = jnp.float32


def _pick(n, prefs):
    for p in prefs:
        if n % p == 0:
            return p
    raise ValueError(f"no tile in {prefs} divides {n}")


def _params(sem):
    return pltpu.CompilerParams(dimension_semantics=sem, vmem_limit_bytes=VMEM_LIMIT_BYTES)


def _rmsnorm_kernel(x_ref, g_ref, o_ref):
    x = x_ref[...]
    var = jnp.mean(x * x, axis=-1, keepdims=True)
    o_ref[...] = (x * lax.rsqrt(var + RMS_EPS) * g_ref[...]).astype(o_ref.dtype)


def _rmsnorm(x, g, out_dtype):
    n, d = x.shape
    tm = _pick(n, (256, 128, 64, 32, 16, 8))
    return pl.pallas_call(
        _rmsnorm_kernel,
        grid=(n // tm,),
        in_specs=[pl.BlockSpec((tm, d), lambda i: (i, 0)),
                  pl.BlockSpec((1, d), lambda i: (0, 0))],
        out_specs=pl.BlockSpec((tm, d), lambda i: (i, 0)),
        out_shape=jax.ShapeDtypeStruct((n, d), out_dtype),
        compiler_params=_params(("parallel",)),
        name="rmsnorm",
    )(x, g.reshape(1, d).astype(F32))


def _mm_kernel(a_ref, b_ref, *rest, epilogue):
    acc = jnp.dot(a_ref[...], b_ref[...], preferred_element_type=F32)
    if epilogue == "relu2":
        (o_ref,) = rest
        r = jnp.maximum(acc, 0.0)
        o_ref[...] = (r * r).astype(o_ref.dtype)
    elif epilogue == "residual":
        r_ref, o_ref = rest
        o_ref[...] = (acc + r_ref[...]).astype(o_ref.dtype)
    else:
        (o_ref,) = rest
        o_ref[...] = acc.astype(o_ref.dtype)


def _matmul(a, b, out_dtype, epilogue="none", residual=None, tm_prefs=(1024, 512, 256, 128, 64, 32, 16, 8),
            tn_prefs=(1024, 512, 256, 128)):
    m, k = a.shape
    _, n = b.shape
    tm = _pick(m, tm_prefs)
    tn = _pick(n, tn_prefs)
    in_specs = [pl.BlockSpec((tm, k), lambda i, j: (i, 0)),
                pl.BlockSpec((k, tn), lambda i, j: (0, j))]
    args = [a, b]
    if epilogue == "residual":
        in_specs.append(pl.BlockSpec((tm, tn), lambda i, j: (i, j)))
        args.append(residual)
    return pl.pallas_call(
        functools.partial(_mm_kernel, epilogue=epilogue),
        grid=(m // tm, n // tn),
        in_specs=in_specs,
        out_specs=pl.BlockSpec((tm, tn), lambda i, j: (i, j)),
        out_shape=jax.ShapeDtypeStruct((m, n), out_dtype),
        compiler_params=_params(("parallel", "arbitrary")),
        name="matmul_" + epilogue,
    )(*args)


def _rope_tile(acc, cos, sin, scale):
    outs = []
    for h in range(acc.shape[1] // HEAD_DIM):
        x = acc[:, h * HEAD_DIM:(h + 1) * HEAD_DIM]
        r = x * cos + pltpu.roll(x, HEAD_DIM // 2, 1) * sin
        if scale != 1.0:
            r = r * scale
        outs.append(r)
    return outs


def _inproj_kernel(a_ref, b_ref, cos_ref, sin_ref, o_ref, *, q_tiles, k_tiles):
    j = pl.program_id(1)
    acc = jnp.dot(a_ref[...], b_ref[...], preferred_element_type=F32)

    def _store_rope(scale):
        outs = _rope_tile(acc, cos_ref[...], sin_ref[...], scale)
        for h, r in enumerate(outs):
            o_ref[:, h * HEAD_DIM:(h + 1) * HEAD_DIM] = r.astype(o_ref.dtype)

    @pl.when(j < q_tiles)
    def _():
        _store_rope(1.0)

    @pl.when(jnp.logical_and(j >= q_tiles, j < q_tiles + k_tiles))
    def _():
        _store_rope(HEAD_DIM ** -0.5)

    @pl.when(j >= q_tiles + k_tiles)
    def _():
        o_ref[...] = acc.astype(o_ref.dtype)


def _inproj(h, w_in, cos_t, sin_t, ret_w, rows_a, seq_a, seq_b):
    m, k = h.shape
    _, n = w_in.shape
    tn = _pick(math.gcd(n, ret_w), (1024, 512, 256, 128))
    tm = _pick(math.gcd(math.gcd(seq_a, seq_b), m), (1024, 512, 256, 128, 64, 32, 16, 8))
    tiles_a = rows_a // tm

    def pos_map(i, j):
        blk = jnp.where(i < tiles_a, i % (seq_a // tm), (i - tiles_a) % (seq_b // tm))
        return (blk, 0)

    return pl.pallas_call(
        functools.partial(_inproj_kernel, q_tiles=ret_w // tn, k_tiles=ret_w // tn),
        grid=(m // tm, n // tn),
        in_specs=[pl.BlockSpec((tm, k), lambda i, j: (i, 0)),
                  pl.BlockSpec((k, tn), lambda i, j: (0, j)),
                  pl.BlockSpec((tm, HEAD_DIM), pos_map),
                  pl.BlockSpec((tm, HEAD_DIM), pos_map)],
        out_specs=pl.BlockSpec((tm, tn), lambda i, j: (i, j)),
        out_shape=jax.ShapeDtypeStruct((m, n), BF16),
        compiler_params=_params(("parallel", "arbitrary")),
        name="inproj_rope",
    )(h, w_in, cos_t, sin_t)


def _ret_kernel(decf_ref, decb_ref, q_ref, k_ref, v_ref, g_ref, o_ref, sb_scr, tab_scr,
                *, nc, cps_a, cps_b, units_a):
    C = RET_CHUNK
    u = pl.program_id(0)
    cps = jnp.where(u < units_a, cps_a, cps_b)

    lgf = -jnp.exp(jnp.broadcast_to(decf_ref[0, 0:1, :], (C, C)))
    lgb = -jnp.exp(jnp.broadcast_to(decb_ref[0, 0:1, :], (C, C)))
    row = lax.broadcasted_iota(jnp.int32, (C, C), 0).astype(F32)
    col = lax.broadcasted_iota(jnp.int32, (C, C), 1).astype(F32)
    diff = row - col
    dmat = jnp.where(diff >= 0, jnp.exp(lgf * jnp.maximum(diff, 0.0)),
                     jnp.exp(lgb * jnp.maximum(-diff, 0.0)))
    tab_scr[0] = dmat
    tab_scr[1] = jnp.exp(lgf * (row + 1.0))
    tab_scr[2] = jnp.exp(lgb * (C - row))
    tab_scr[3] = jnp.exp(lgf * (C - 1.0 - row))
    tab_scr[4] = jnp.exp(lgb * row)
    cdf = jnp.exp(lgf * C)
    cdb = jnp.exp(lgb * C)

    tdims = (((0,), (0,)), ((), ()))
    ndims = (((1,), (1,)), ((), ()))

    def bwd_body(t, sb):
        c = nc - 1 - t
        sb = jnp.where((c + 1) % cps == 0, 0.0, sb)
        sb_scr[c] = sb.astype(BF16)
        rows = pl.ds(pl.multiple_of(c * C, C), C)
        kz = (k_ref[rows, :].astype(F32) * tab_scr[4]).astype(BF16)
        kv = lax.dot_general(kz, v_ref[rows, :], tdims, preferred_element_type=F32)
        return sb * cdb + kv

    lax.fori_loop(0, nc, bwd_body, jnp.zeros((C, C), F32))

    def fwd_body(c, sf):
        sf = jnp.where(c % cps == 0, 0.0, sf)
        rows = pl.ds(pl.multiple_of(c * C, C), C)
        q = q_ref[rows, :]
        k = k_ref[rows, :]
        v = v_ref[rows, :]
        s = lax.dot_general(q, k, ndims, preferred_element_type=F32)
        att = (s * tab_scr[0]).astype(BF16)
        qf = q.astype(F32)
        qx = jnp.concatenate([(qf * tab_scr[1]).astype(BF16), (qf * tab_scr[2]).astype(BF16)], axis=1)
        scat = jnp.concatenate([sf.astype(BF16), sb_scr[c]], axis=0)
        y = jnp.dot(att, v, preferred_element_type=F32) + jnp.dot(qx, scat, preferred_element_type=F32)
        kz = (k.astype(F32) * tab_scr[3]).astype(BF16)
        sf_new = sf * cdf + lax.dot_general(kz, v, tdims, preferred_element_type=F32)
        var = jnp.mean(y * y, axis=-1, keepdims=True)
        g = g_ref[rows, :].astype(F32)
        gate = g * (1.0 / (1.0 + jnp.exp(-g)))
        o_ref[rows, :] = (y * lax.rsqrt(var + RMS_EPS) * gate).astype(o_ref.dtype)
        return sf_new

    lax.fori_loop(0, nc, fwd_body, jnp.zeros((C, C), F32))


def _retention(proj, dec_f, dec_b, ret_heads, unit, units_a, seq_a, seq_b):
    n = proj.shape[0]
    nc = unit // RET_CHUNK
    dec_f = jnp.broadcast_to(dec_f.astype(F32).reshape(ret_heads, 1, 1), (ret_heads, 8, HEAD_DIM))
    dec_b = jnp.broadcast_to(dec_b.astype(F32).reshape(ret_heads, 1, 1), (ret_heads, 8, HEAD_DIM))
    dec_spec = pl.BlockSpec((1, 8, HEAD_DIM), lambda u, h: (h, 0, 0))

    def col_spec(base):
        return pl.BlockSpec((unit, HEAD_DIM), lambda u, h: (u, base + h))

    return pl.pallas_call(
        functools.partial(_ret_kernel, nc=nc, cps_a=seq_a // RET_CHUNK, cps_b=seq_b // RET_CHUNK,
                          units_a=units_a),
        grid=(n // unit, ret_heads),
        in_specs=[dec_spec, dec_spec, col_spec(0), col_spec(ret_heads), col_spec(2 * ret_heads),
                  col_spec(3 * ret_heads)],
        out_specs=pl.BlockSpec((unit, HEAD_DIM), lambda u, h: (u, h)),
        out_shape=jax.ShapeDtypeStruct((n, ret_heads * HEAD_DIM), BF16),
        scratch_shapes=[pltpu.VMEM((nc, RET_CHUNK, RET_CHUNK), BF16),
                        pltpu.VMEM((5, RET_CHUNK, RET_CHUNK), F32)],
        compiler_params=_params(("parallel", "arbitrary")),
        name="retention",
    )(dec_f, dec_b, proj, proj, proj, proj)


def _swa_kernel(sink_ref, q_ref, k_ref, v_ref, bias_ref, o_ref, *, nblk, bps_a, bps_b, units_a):
    B = BLOCK
    u = pl.program_id(0)
    kvh = pl.program_id(1)
    n = pl.program_id(2)
    bps = jnp.where(u < units_a, bps_a, bps_b)
    has_prev = (n % bps) != 0
    has_next = ((n + 1) % bps) != 0
    rp = pl.ds(pl.multiple_of(jnp.maximum(n - 1, 0) * B, B), B)
    rc = pl.ds(pl.multiple_of(n * B, B), B)
    rn = pl.ds(pl.multiple_of(jnp.minimum(n + 1, nblk - 1) * B, B), B)
    kband = jnp.concatenate([k_ref[rp, :], k_ref[rc, :], k_ref[rn, :]], axis=0)
    vband = jnp.concatenate([v_ref[rp, :], v_ref[rc, :], v_ref[rn, :]], axis=0)
    row = lax.broadcasted_iota(jnp.int32, (B, 3 * B), 0)
    col = lax.broadcasted_iota(jnp.int32, (B, 3 * B), 1)
    rel = col - B - row
    ok = jnp.logical_and(jnp.abs(rel) <= WINDOW,
                         jnp.logical_and(jnp.logical_or(col >= B, has_prev),
                                         jnp.logical_or(col < 2 * B, has_next)))
    ndims = (((1,), (1,)), ((), ()))
    for g in range(GQA_GROUP):
        cols = slice(g * HEAD_DIM, (g + 1) * HEAD_DIM)
        sink = sink_ref[kvh * GQA_GROUP + g]
        s = lax.dot_general(q_ref[:, cols], kband, ndims, preferred_element_type=F32)
        s = jnp.where(ok, s * (HEAD_DIM ** -0.5) + bias_ref[g], NEG_INF)
        m = jnp.maximum(jnp.max(s, axis=-1, keepdims=True), sink)
        p = jnp.exp(s - m)
        denom = jnp.sum(p, axis=-1, keepdims=True) + jnp.exp(sink - m)
        o = jnp.dot(p.astype(BF16), vband, preferred_element_type=F32) * (1.0 / denom)
        o_ref[:, cols] = o.astype(o_ref.dtype)


def _swa(proj, sink, bias, ret_w, att_heads, unit, units_a, seq_a, seq_b):
    n = proj.shape[0]
    kv_heads = att_heads // GQA_GROUP
    nblk = unit // BLOCK
    gw = GQA_GROUP * HEAD_DIM
    q_base = 4 * ret_w // gw
    k_base = (4 * ret_w + att_heads * HEAD_DIM) // HEAD_DIM
    v_base = k_base + kv_heads
    return pl.pallas_call(
        functools.partial(_swa_kernel, nblk=nblk, bps_a=seq_a // BLOCK, bps_b=seq_b // BLOCK, units_a=units_a),
        grid=(n // unit, kv_heads, nblk),
        in_specs=[pl.BlockSpec(memory_space=pltpu.SMEM),
                  pl.BlockSpec((BLOCK, gw), lambda u, kv, i: (u * nblk + i, q_base + kv)),
                  pl.BlockSpec((unit, HEAD_DIM), lambda u, kv, i: (u, k_base + kv)),
                  pl.BlockSpec((unit, HEAD_DIM), lambda u, kv, i: (u, v_base + kv)),
                  pl.BlockSpec((GQA_GROUP, BLOCK, 3 * BLOCK), lambda u, kv, i: (kv, 0, 0))],
        out_specs=pl.BlockSpec((BLOCK, gw), lambda u, kv, i: (u * nblk + i, kv)),
        out_shape=jax.ShapeDtypeStruct((n, att_heads * HEAD_DIM), BF16),
        compiler_params=_params(("parallel", "arbitrary", "arbitrary")),
        name="swa",
    )(sink.astype(F32), proj, proj, proj, bias)


def _t5_buckets(rel):
    nb = N_BUCKETS // 2
    max_exact = nb // 2
    base = np.where(rel > 0, nb, 0)
    n = np.abs(rel)
    large = max_exact + (np.log(np.maximum(n, 1) / max_exact) / np.log(MAX_DISTANCE / max_exact)
                         * (nb - max_exact)).astype(np.int32)
    large = np.minimum(large, nb - 1)
    return (base + np.where(n < max_exact, n, large)).astype(np.int32)


def _band_bias(rel_bias):
    qi = np.arange(BLOCK)[:, None]
    kj = np.arange(3 * BLOCK)[None, :] - BLOCK
    buckets = _t5_buckets(kj - qi)
    return jnp.transpose(rel_bias.astype(F32)[buckets], (2, 0, 1))


def _xattn_kernel(q_ref, kv_ref, o_ref):
    ndims = (((1,), (1,)), ((), ()))
    mem_w = MEM_HEADS * HEAD_DIM
    for h in range(MEM_HEADS):
        cols = slice(h * HEAD_DIM, (h + 1) * HEAD_DIM)
        s = lax.dot_general(q_ref[:, cols], kv_ref[:, cols], ndims, preferred_element_type=F32)
        s = s * (HEAD_DIM ** -0.5)
        m = jnp.max(s, axis=-1, keepdims=True)
        p = jnp.exp(s - m)
        denom = jnp.sum(p, axis=-1, keepdims=True)
        v = kv_ref[:, mem_w + h * HEAD_DIM:mem_w + (h + 1) * HEAD_DIM]
        o = jnp.dot(p.astype(BF16), v, preferred_element_type=F32) * (1.0 / denom)
        o_ref[:, cols] = o.astype(o_ref.dtype)


def _xattn(q, kvm, mem_tokens, rows_a, seq_a, seq_b):
    n, mem_w = q.shape
    tm = _pick(math.gcd(seq_a, seq_b), (512, 256, 128, 64, 32, 16, 8))
    tiles_a = rows_a // tm
    batch_a = rows_a // seq_a

    def kv_map(i):
        b = jnp.where(i < tiles_a, i // (seq_a // tm), batch_a + (i - tiles_a) // (seq_b // tm))
        return (b, 0)

    return pl.pallas_call(
        _xattn_kernel,
        grid=(n // tm,),
        in_specs=[pl.BlockSpec((tm, mem_w), lambda i: (i, 0)),
                  pl.BlockSpec((mem_tokens, 2 * mem_w), kv_map)],
        out_specs=pl.BlockSpec((tm, mem_w), lambda i: (i, 0)),
        out_shape=jax.ShapeDtypeStruct((n, mem_w), BF16),
        compiler_params=_params(("parallel",)),
        name="xattn",
    )(q, kvm)


def _outproj_kernel(a1_ref, a2_ref, b1_ref, b2_ref, r_ref, o_ref):
    acc = jnp.dot(a1_ref[...], b1_ref[...], preferred_element_type=F32)
    acc = acc + jnp.dot(a2_ref[...], b2_ref[...], preferred_element_type=F32)
    o_ref[...] = acc + r_ref[...]


def _outproj(y_ret, y_att, w_out, x):
    m, k1 = y_ret.shape
    _, k2 = y_att.shape
    n = w_out.shape[1]
    tm = _pick(m, (1024, 512, 256, 128, 64, 32, 16, 8))
    tn = _pick(math.gcd(n, math.gcd(k1, k2)), (1024, 512, 256, 128))
    return pl.pallas_call(
        _outproj_kernel,
        grid=(m // tm, n // tn),
        in_specs=[pl.BlockSpec((tm, k1), lambda i, j: (i, 0)),
                  pl.BlockSpec((tm, k2), lambda i, j: (i, 0)),
                  pl.BlockSpec((k1, tn), lambda i, j: (0, j)),
                  pl.BlockSpec((k2, tn), lambda i, j: (k1 // k2, j)),
                  pl.BlockSpec((tm, tn), lambda i, j: (i, j))],
        out_specs=pl.BlockSpec((tm, tn), lambda i, j: (i, j)),
        out_shape=jax.ShapeDtypeStruct((m, n), F32),
        compiler_params=_params(("parallel", "arbitrary")),
        name="outproj",
    )(y_ret, y_att, w_out, w_out, x)


def _mm_kacc_kernel(a_ref, b_ref, r_ref, o_ref):
    kk = pl.program_id(1)
    part = jnp.dot(a_ref[...], b_ref[...], preferred_element_type=F32)

    @pl.when(kk == 0)
    def _():
        o_ref[...] = r_ref[...] + part

    @pl.when(kk != 0)
    def _():
        o_ref[...] += part


def _matmul_kacc(a, b, residual):
    m, k = a.shape
    _, n = b.shape
    tm = _pick(m, (512, 256, 128, 64, 32, 16, 8))
    tk = _pick(k, (512, 256, 128))
    return pl.pallas_call(
        _mm_kacc_kernel,
        grid=(m // tm, k // tk),
        in_specs=[pl.BlockSpec((tm, tk), lambda i, kk: (i, kk)),
                  pl.BlockSpec((tk, n), lambda i, kk: (kk, 0)),
                  pl.BlockSpec((tm, n), lambda i, kk: (i, 0))],
        out_specs=pl.BlockSpec((tm, n), lambda i, kk: (i, 0)),
        out_shape=jax.ShapeDtypeStruct((m, n), F32),
        compiler_params=_params(("parallel", "arbitrary")),
        name="matmul_kacc",
    )(a, b, residual)


def _rope_tables(seq):
    half = HEAD_DIM // 2
    inv = ROPE_BASE ** (-jnp.arange(half, dtype=F32) / half)
    ang = jnp.arange(seq, dtype=F32)[:, None] * inv[None, :]
    cos = jnp.cos(ang)
    sin = jnp.sin(ang)
    return jnp.concatenate([cos, cos], axis=-1), jnp.concatenate([-sin, sin], axis=-1)


def kernel(x_prompt, x_sample, mem_prompt, mem_sample, norm_mix, w_in, ret_decay_f, ret_decay_b, attn_sink,
           rel_bias, w_out, norm_cross, norm_mem, w_cq, w_ckv, w_co, norm_mlp, w_mlp_in, w_mlp_out, norm_final):
    batch_a, seq_a, d = x_prompt.shape
    batch_b, seq_b, _ = x_sample.shape
    depth = w_in.shape[0]
    ret_heads = ret_decay_f.shape[1]
    att_heads = attn_sink.shape[1]
    ret_w = ret_heads * HEAD_DIM
    mem_tokens = mem_prompt.shape[1]
    rows_a = batch_a * seq_a
    rows_b = batch_b * seq_b
    unit = max(seq_a, seq_b)
    assert unit % seq_a == 0 and unit % seq_b == 0 and rows_a % unit == 0 and rows_b % unit == 0
    units_a = rows_a // unit

    x = jnp.concatenate([x_prompt.reshape(rows_a, d), x_sample.reshape(rows_b, d)], axis=0)
    mem = jnp.concatenate([mem_prompt.reshape(batch_a * mem_tokens, d),
                           mem_sample.reshape(batch_b * mem_tokens, d)], axis=0)
    cos_t, sin_t = _rope_tables(unit)
    bias = _band_bias(rel_bias)

    for l in range(depth):
        h = _rmsnorm(x, norm_mix[l], BF16)
        proj = _inproj(h, w_in[l].astype(BF16), cos_t, sin_t, ret_w, rows_a, seq_a, seq_b)
        y_ret = _retention(proj, ret_decay_f[l], ret_decay_b[l], ret_heads, unit, units_a, seq_a, seq_b)
        y_att = _swa(proj, attn_sink[l], bias, ret_w, att_heads, unit, units_a, seq_a, seq_b)
        x = _outproj(y_ret, y_att, w_out[l].astype(BF16), x)

        h = _rmsnorm(x, norm_cross[l], BF16)
        hm = _rmsnorm(mem, norm_mem[l], BF16)
        q = _matmul(h, w_cq[l].astype(BF16), BF16)
        kvm = _matmul(hm, w_ckv[l].astype(BF16), BF16)
        o = _xattn(q, kvm, mem_tokens, rows_a, seq_a, seq_b)
        x = _matmul(o, w_co[l].astype(BF16), F32, epilogue="residual", residual=x)

        h = _rmsnorm(x, norm_mlp[l], BF16)
        a = _matmul(h, w_mlp_in[l].astype(BF16), BF16, epilogue="relu2")
        x = _matmul_kacc(a, w_mlp_out[l].astype(BF16), x)

    y = _rmsnorm(x, norm_final, F32)
    return (y[:rows_a].reshape(batch_a, seq_a, d), y[rows_a:].reshape(batch_b, seq_b, d))
```

```python
import functools
import math

import numpy as np
import jax
import jax.numpy as jnp
from jax import lax
from jax.experimental import pallas as pl
from jax.experimental.pallas import tpu as pltpu

HEAD_DIM = 128
RET_CHUNK = 128
WINDOW = 128
BLOCK = 128
N_BUCKETS = 32
MAX_DISTANCE = 128
MEM_HEADS = 4
GQA_GROUP = 4
ROPE_BASE = 10000.0
RMS_EPS = 1e-6
NEG_INF = -1e30

VMEM_LIMIT_BYTES = 56 * 1024 * 1024

BF16 = jnp.bfloat16
F32 = jnp.float32

ROW_TILES = (1024, 512, 256, 128, 64, 32, 16, 8)
COL_TILES = (1024, 512, 256, 128)


def _pick(n, prefs):
    for p in prefs:
        if n % p == 0:
            return p
    raise ValueError(f"no tile in {prefs} divides {n}")


def _params(sem):
    return pltpu.CompilerParams(dimension_semantics=sem, vmem_limit_bytes=VMEM_LIMIT_BYTES)


def _rmsnorm_kernel(x_ref, g_ref, o_ref):
    x = x_ref[...]
    var = jnp.mean(x * x, axis=-1, keepdims=True)
    o_ref[...] = (x * lax.rsqrt(var + RMS_EPS) * g_ref[...]).astype(o_ref.dtype)


def _rmsnorm(x, g, out_dtype):
    n, d = x.shape
    tm = _pick(n, (256, 128, 64, 32, 16, 8))
    return pl.pallas_call(
        _rmsnorm_kernel,
        grid=(n // tm,),
        in_specs=[pl.BlockSpec((tm, d), lambda i: (i, 0)),
                  pl.BlockSpec((1, d), lambda i: (0, 0))],
        out_specs=pl.BlockSpec((tm, d), lambda i: (i, 0)),
        out_shape=jax.ShapeDtypeStruct((n, d), out_dtype),
        compiler_params=_params(("parallel",)),
        name="rmsnorm",
    )(x, g.reshape(1, d).astype(F32))


def _mm_kernel(a_ref, b_ref, *rest, epilogue):
    acc = jnp.dot(a_ref[...], b_ref[...], preferred_element_type=F32)
    if epilogue == "relu2":
        (o_ref,) = rest
        r = jnp.maximum(acc, 0.0)
        o_ref[...] = (r * r).astype(o_ref.dtype)
    elif epilogue == "residual":
        r_ref, o_ref = rest
        o_ref[...] = (acc + r_ref[...]).astype(o_ref.dtype)
    else:
        (o_ref,) = rest
        o_ref[...] = acc.astype(o_ref.dtype)


def _matmul(a, b, out_dtype, epilogue="none", residual=None):
    m, k = a.shape
    _, n = b.shape
    tm = _pick(m, ROW_TILES)
    tn = _pick(n, COL_TILES)
    in_specs = [pl.BlockSpec((tm, k), lambda i, j: (i, 0)),
                pl.BlockSpec((k, tn), lambda i, j: (0, j))]
    args = [a, b]
    if epilogue == "residual":
        in_specs.append(pl.BlockSpec((tm, tn), lambda i, j: (i, j)))
        args.append(residual)
    return pl.pallas_call(
        functools.partial(_mm_kernel, epilogue=epilogue),
        grid=(m // tm, n // tn),
        in_specs=in_specs,
        out_specs=pl.BlockSpec((tm, tn), lambda i, j: (i, j)),
        out_shape=jax.ShapeDtypeStruct((m, n), out_dtype),
        compiler_params=_params(("parallel", "arbitrary")),
        name="matmul_" + epilogue,
    )(*args)


def _inproj_kernel(a_ref, b_ref, cos_ref, sin_ref, o_ref, *, q_tiles, k_tiles):
    j = pl.program_id(1)
    acc = jnp.dot(a_ref[...], b_ref[...], preferred_element_type=F32)

    def _store_rope(scale):
        cos = cos_ref[...]
        sin = sin_ref[...]
        for h in range(acc.shape[1] // HEAD_DIM):
            cols = slice(h * HEAD_DIM, (h + 1) * HEAD_DIM)
            x = acc[:, cols]
            r = x * cos + pltpu.roll(x, HEAD_DIM // 2, 1) * sin
            if scale != 1.0:
                r = r * scale
            o_ref[:, cols] = r.astype(o_ref.dtype)

    @pl.when(j < q_tiles)
    def _():
        _store_rope(1.0)

    @pl.when(jnp.logical_and(j >= q_tiles, j < q_tiles + k_tiles))
    def _():
        _store_rope(HEAD_DIM ** -0.5)

    @pl.when(j >= q_tiles + k_tiles)
    def _():
        o_ref[...] = acc.astype(o_ref.dtype)


def _inproj(h, w_in, cos_t, sin_t, ret_w, seq):
    m, k = h.shape
    _, n = w_in.shape
    tn = _pick(math.gcd(n, ret_w), COL_TILES)
    tm = _pick(seq, ROW_TILES)
    pos_spec = pl.BlockSpec((tm, HEAD_DIM), lambda i, j: (i % (seq // tm), 0))
    return pl.pallas_call(
        functools.partial(_inproj_kernel, q_tiles=ret_w // tn, k_tiles=ret_w // tn),
        grid=(m // tm, n // tn),
        in_specs=[pl.BlockSpec((tm, k), lambda i, j: (i, 0)),
                  pl.BlockSpec((k, tn), lambda i, j: (0, j)),
                  pos_spec, pos_spec],
        out_specs=pl.BlockSpec((tm, tn), lambda i, j: (i, j)),
        out_shape=jax.ShapeDtypeStruct((m, n), BF16),
        compiler_params=_params(("parallel", "arbitrary")),
        name="inproj_rope",
    )(h, w_in, cos_t, sin_t)


RET_MAX_UNROLL = 4


def _ret_kernel(decf_ref, decb_ref, q_ref, k_ref, v_ref, g_ref, o_ref, st_scr, tab_scr, *, nc):
    C = RET_CHUNK
    RET_UNROLL = math.gcd(nc, RET_MAX_UNROLL)
    lgf = -jnp.exp(jnp.broadcast_to(decf_ref[0, 0:1, :], (C, C)))
    lgb = -jnp.exp(jnp.broadcast_to(decb_ref[0, 0:1, :], (C, C)))
    row = lax.broadcasted_iota(jnp.int32, (C, C), 0).astype(F32)
    col = lax.broadcasted_iota(jnp.int32, (C, C), 1).astype(F32)
    diff = row - col
    tab_scr[0] = jnp.where(diff >= 0, jnp.exp(lgf * jnp.maximum(diff, 0.0)),
                           jnp.exp(lgb * jnp.maximum(-diff, 0.0)))
    tab_scr[1] = jnp.exp(lgf * (row + 1.0))
    tab_scr[2] = jnp.exp(lgb * (C - row))
    tab_scr[3] = jnp.exp(lgf * (C - 1.0 - row))
    tab_scr[4] = jnp.exp(lgb * row)
    cdf = jnp.exp(lgf * C)
    cdb = jnp.exp(lgb * C)

    tdims = (((0,), (0,)), ((), ()))
    ndims = (((1,), (1,)), ((), ()))

    def chunk_rows(c):
        return pl.ds(pl.multiple_of(c * C, C), C)

    def scan_body(t, carry):
        sf, sb = carry
        cf = t
        cb = nc - 1 - t
        st_scr[cf, 0:C, :] = sf.astype(BF16)
        st_scr[cb, C:2 * C, :] = sb.astype(BF16)
        rf = chunk_rows(cf)
        rb = chunk_rows(cb)
        kzf = (k_ref[rf, :].astype(F32) * tab_scr[3]).astype(BF16)
        kzb = (k_ref[rb, :].astype(F32) * tab_scr[4]).astype(BF16)
        sf = sf * cdf + lax.dot_general(kzf, v_ref[rf, :], tdims, preferred_element_type=F32)
        sb = sb * cdb + lax.dot_general(kzb, v_ref[rb, :], tdims, preferred_element_type=F32)
        return sf, sb

    zero = jnp.zeros((C, C), F32)
    lax.fori_loop(0, nc, scan_body, (zero, zero), unroll=RET_UNROLL)

    def out_body(t, carry):
        cs = [t * RET_UNROLL + i for i in range(RET_UNROLL)]
        rows = [chunk_rows(c) for c in cs]
        qs = [q_ref[r, :] for r in rows]
        ss = [lax.dot_general(q, k_ref[r, :], ndims, preferred_element_type=F32) for q, r in zip(qs, rows)]
        atts = [(s * tab_scr[0]).astype(BF16) for s in ss]
        qxs = []
        for q in qs:
            qf = q.astype(F32)
            qxs.append(jnp.concatenate([(qf * tab_scr[1]).astype(BF16), (qf * tab_scr[2]).astype(BF16)], axis=1))
        ys = [jnp.dot(att, v_ref[r, :], preferred_element_type=F32)
              + jnp.dot(qx, st_scr[c], preferred_element_type=F32)
              for att, qx, r, c in zip(atts, qxs, rows, cs)]
        for y, r in zip(ys, rows):
            var = jnp.mean(y * y, axis=-1, keepdims=True)
            g = g_ref[r, :].astype(F32)
            gate = g * (1.0 / (1.0 + jnp.exp(-g)))
            o_ref[r, :] = (y * lax.rsqrt(var + RMS_EPS) * gate).astype(o_ref.dtype)
        return carry

    lax.fori_loop(0, nc // RET_UNROLL, out_body, 0)


def _retention(proj, dec_f, dec_b, ret_heads, seq):
    n = proj.shape[0]
    nc = seq // RET_CHUNK
    dec_f = jnp.broadcast_to(dec_f.astype(F32).reshape(ret_heads, 1, 1), (ret_heads, 8, HEAD_DIM))
    dec_b = jnp.broadcast_to(dec_b.astype(F32).reshape(ret_heads, 1, 1), (ret_heads, 8, HEAD_DIM))
    dec_spec = pl.BlockSpec((1, 8, HEAD_DIM), lambda b, h: (h, 0, 0))

    def col_spec(base):
        return pl.BlockSpec((seq, HEAD_DIM), lambda b, h: (b, base + h))

    return pl.pallas_call(
        functools.partial(_ret_kernel, nc=nc),
        grid=(n // seq, ret_heads),
        in_specs=[dec_spec, dec_spec, col_spec(0), col_spec(ret_heads), col_spec(2 * ret_heads),
                  col_spec(3 * ret_heads)],
        out_specs=pl.BlockSpec((seq, HEAD_DIM), lambda b, h: (b, h)),
        out_shape=jax.ShapeDtypeStruct((n, ret_heads * HEAD_DIM), BF16),
        scratch_shapes=[pltpu.VMEM((nc, 2 * RET_CHUNK, RET_CHUNK), BF16),
                        pltpu.VMEM((5, RET_CHUNK, RET_CHUNK), F32)],
        compiler_params=_params(("parallel", "arbitrary")),
        name="retention",
    )(dec_f, dec_b, proj, proj, proj, proj)


SWA_QBLOCKS = 2


def _swa_kernel(sink_ref, q_ref, k_ref, v_ref, bias_ref, o_ref, *, nblk):
    B = BLOCK
    kvh = pl.program_id(1)
    step = pl.program_id(2)
    row = lax.broadcasted_iota(jnp.int32, (B, 3 * B), 0)
    col = lax.broadcasted_iota(jnp.int32, (B, 3 * B), 1)
    in_window = jnp.abs(col - B - row) <= WINDOW
    ndims = (((1,), (1,)), ((), ()))
    bands = []
    scores = []
    for qb in range(SWA_QBLOCKS):
        n = step * SWA_QBLOCKS + qb
        has_prev = n > 0
        has_next = n < nblk - 1
        rp = pl.ds(pl.multiple_of(jnp.maximum(n - 1, 0) * B, B), B)
        rc = pl.ds(pl.multiple_of(n * B, B), B)
        rn = pl.ds(pl.multiple_of(jnp.minimum(n + 1, nblk - 1) * B, B), B)
        kband = jnp.concatenate([k_ref[rp, :], k_ref[rc, :], k_ref[rn, :]], axis=0)
        vband = jnp.concatenate([v_ref[rp, :], v_ref[rc, :], v_ref[rn, :]], axis=0)
        ok = jnp.logical_and(in_window,
                             jnp.logical_and(jnp.logical_or(col >= B, has_prev),
                                             jnp.logical_or(col < 2 * B, has_next)))
        bands.append((vband, ok))
        for g in range(GQA_GROUP):
            q = q_ref[qb * B:(qb + 1) * B, g * HEAD_DIM:(g + 1) * HEAD_DIM]
            scores.append(lax.dot_general(q, kband, ndims, preferred_element_type=F32))
    probs = []
    for idx, s in enumerate(scores):
        qb, g = divmod(idx, GQA_GROUP)
        sink = sink_ref[kvh * GQA_GROUP + g]
        s = jnp.where(bands[qb][1], s * (HEAD_DIM ** -0.5) + bias_ref[g], NEG_INF)
        m = jnp.maximum(jnp.max(s, axis=-1, keepdims=True), sink)
        p = jnp.exp(s - m)
        denom = jnp.sum(p, axis=-1, keepdims=True) + jnp.exp(sink - m)
        probs.append((p.astype(BF16), 1.0 / denom))
    for idx, (p, inv) in enumerate(probs):
        qb, g = divmod(idx, GQA_GROUP)
        o = jnp.dot(p, bands[qb][0], preferred_element_type=F32) * inv
        o_ref[qb * B:(qb + 1) * B, g * HEAD_DIM:(g + 1) * HEAD_DIM] = o.astype(o_ref.dtype)


def _swa(proj, sink, bias, ret_w, att_heads, seq):
    n = proj.shape[0]
    kv_heads = att_heads // GQA_GROUP
    nblk = seq // BLOCK
    assert nblk % SWA_QBLOCKS == 0
    steps = nblk // SWA_QBLOCKS
    qrows = SWA_QBLOCKS * BLOCK
    gw = GQA_GROUP * HEAD_DIM
    q_base = 4 * ret_w // gw
    k_base = (4 * ret_w + att_heads * HEAD_DIM) // HEAD_DIM
    v_base = k_base + kv_heads
    return pl.pallas_call(
        functools.partial(_swa_kernel, nblk=nblk),
        grid=(n // seq, kv_heads, steps),
        in_specs=[pl.BlockSpec(memory_space=pltpu.SMEM),
                  pl.BlockSpec((qrows, gw), lambda b, kv, i: (b * steps + i, q_base + kv)),
                  pl.BlockSpec((seq, HEAD_DIM), lambda b, kv, i: (b, k_base + kv)),
                  pl.BlockSpec((seq, HEAD_DIM), lambda b, kv, i: (b, v_base + kv)),
                  pl.BlockSpec((GQA_GROUP, BLOCK, 3 * BLOCK), lambda b, kv, i: (kv, 0, 0))],
        out_specs=pl.BlockSpec((qrows, gw), lambda b, kv, i: (b * steps + i, kv)),
        out_shape=jax.ShapeDtypeStruct((n, att_heads * HEAD_DIM), BF16),
        compiler_params=_params(("parallel", "arbitrary", "arbitrary")),
        name="swa",
    )(sink.astype(F32), proj, proj, proj, bias)


def _t5_buckets(rel):
    nb = N_BUCKETS // 2
    max_exact = nb // 2
    base = np.where(rel > 0, nb, 0)
    n = np.abs(rel)
    large = max_exact + (np.log(np.maximum(n, 1) / max_exact) / np.log(MAX_DISTANCE / max_exact)
                         * (nb - max_exact)).astype(np.int32)
    large = np.minimum(large, nb - 1)
    return (base + np.where(n < max_exact, n, large)).astype(np.int32)


def _band_bias(rel_bias):
    qi = np.arange(BLOCK)[:, None]
    kj = np.arange(3 * BLOCK)[None, :] - BLOCK
    buckets = jnp.asarray(_t5_buckets(kj - qi).reshape(-1, 1))
    onehot = (buckets == jnp.arange(N_BUCKETS, dtype=jnp.int32)[None, :]).astype(F32)
    table = jnp.dot(onehot, rel_bias.astype(F32), precision=lax.Precision.HIGHEST)
    return jnp.transpose(table).reshape(rel_bias.shape[1], BLOCK, 3 * BLOCK)


def _xattn_kernel(q_ref, kv_ref, o_ref):
    ndims = (((1,), (1,)), ((), ()))
    mem_w = MEM_HEADS * HEAD_DIM
    for h in range(MEM_HEADS):
        cols = slice(h * HEAD_DIM, (h + 1) * HEAD_DIM)
        s = lax.dot_general(q_ref[:, cols], kv_ref[:, cols], ndims, preferred_element_type=F32)
        s = s * (HEAD_DIM ** -0.5)
        m = jnp.max(s, axis=-1, keepdims=True)
        p = jnp.exp(s - m)
        denom = jnp.sum(p, axis=-1, keepdims=True)
        v = kv_ref[:, mem_w + h * HEAD_DIM:mem_w + (h + 1) * HEAD_DIM]
        o = jnp.dot(p.astype(BF16), v, preferred_element_type=F32) * (1.0 / denom)
        o_ref[:, cols] = o.astype(o_ref.dtype)


def _xattn(q, kvm, mem_tokens, seq):
    n, mem_w = q.shape
    tm = _pick(seq, (512, 256, 128, 64, 32, 16, 8))
    return pl.pallas_call(
        _xattn_kernel,
        grid=(n // tm,),
        in_specs=[pl.BlockSpec((tm, mem_w), lambda i: (i, 0)),
                  pl.BlockSpec((mem_tokens, 2 * mem_w), lambda i: (i // (seq // tm), 0))],
        out_specs=pl.BlockSpec((tm, mem_w), lambda i: (i, 0)),
        out_shape=jax.ShapeDtypeStruct((n, mem_w), BF16),
        compiler_params=_params(("parallel",)),
        name="xattn",
    )(q, kvm)


def _outproj_kernel(a1_ref, a2_ref, b1_ref, b2_ref, r_ref, o_ref):
    acc = jnp.dot(a1_ref[...], b1_ref[...], preferred_element_type=F32)
    acc = acc + jnp.dot(a2_ref[...], b2_ref[...], preferred_element_type=F32)
    o_ref[...] = acc + r_ref[...]


def _outproj(y_ret, y_att, w_out, x):
    m, k1 = y_ret.shape
    _, k2 = y_att.shape
    assert k1 % k2 == 0
    n = w_out.shape[1]
    tm = _pick(m, ROW_TILES)
    tn = _pick(n, COL_TILES)
    return pl.pallas_call(
        _outproj_kernel,
        grid=(m // tm, n // tn),
        in_specs=[pl.BlockSpec((tm, k1), lambda i, j: (i, 0)),
                  pl.BlockSpec((tm, k2), lambda i, j: (i, 0)),
                  pl.BlockSpec((k1, tn), lambda i, j: (0, j)),
                  pl.BlockSpec((k2, tn), lambda i, j: (k1 // k2, j)),
                  pl.BlockSpec((tm, tn), lambda i, j: (i, j))],
        out_specs=pl.BlockSpec((tm, tn), lambda i, j: (i, j)),
        out_shape=jax.ShapeDtypeStruct((m, n), F32),
        compiler_params=_params(("parallel", "arbitrary")),
        name="outproj",
    )(y_ret, y_att, w_out, w_out, x)


def _mm_kacc_kernel(a_ref, b_ref, r_ref, o_ref):
    kk = pl.program_id(2)

    @pl.when(kk == 0)
    def _():
        o_ref[...] = r_ref[...] + jnp.dot(a_ref[...], b_ref[...], preferred_element_type=F32)

    @pl.when(kk != 0)
    def _():
        o_ref[...] += jnp.dot(a_ref[...], b_ref[...], preferred_element_type=F32)


def _matmul_kacc(a, b, residual):
    m, k = a.shape
    _, n = b.shape
    tm = _pick(m, ROW_TILES)
    tn = _pick(n, COL_TILES)
    tk = _pick(k, (2048, 1024, 512, 256, 128))
    return pl.pallas_call(
        _mm_kacc_kernel,
        grid=(m // tm, n // tn, k // tk),
        in_specs=[pl.BlockSpec((tm, tk), lambda i, j, kk: (i, kk)),
                  pl.BlockSpec((tk, tn), lambda i, j, kk: (kk, j)),
                  pl.BlockSpec((tm, tn), lambda i, j, kk: (i, j))],
        out_specs=pl.BlockSpec((tm, tn), lambda i, j, kk: (i, j)),
        out_shape=jax.ShapeDtypeStruct((m, n), F32),
        compiler_params=_params(("parallel", "arbitrary", "arbitrary")),
        name="matmul_kacc",
    )(a, b, residual)


def _rope_tables(seq):
    half = HEAD_DIM // 2
    inv = ROPE_BASE ** (-jnp.arange(half, dtype=F32) / half)
    ang = jnp.arange(seq, dtype=F32)[:, None] * inv[None, :]
    cos = jnp.cos(ang)
    sin = jnp.sin(ang)
    return jnp.concatenate([cos, cos], axis=-1), jnp.concatenate([-sin, sin], axis=-1)


def _encode(x3d, mem3d, w, bias):
    batch, seq, d = x3d.shape
    mem_tokens = mem3d.shape[1]
    x = x3d.reshape(batch * seq, d)
    mem = mem3d.reshape(batch * mem_tokens, d)
    cos_t, sin_t = _rope_tables(seq)
    for l in range(w["depth"]):
        ret_heads = w["dec_f"][l].shape[0]
        att_heads = w["sink"][l].shape[0]
        ret_w = ret_heads * HEAD_DIM
        h = _rmsnorm(x, w["norm_mix"][l], BF16)
        proj = _inproj(h, w["w_in"][l], cos_t, sin_t, ret_w, seq)
        y_ret = _retention(proj, w["dec_f"][l], w["dec_b"][l], ret_heads, seq)
        y_att = _swa(proj, w["sink"][l], bias, ret_w, att_heads, seq)
        x = _outproj(y_ret, y_att, w["w_out"][l], x)

        h = _rmsnorm(x, w["norm_cross"][l], BF16)
        hm = _rmsnorm(mem, w["norm_mem"][l], BF16)
        q = _matmul(h, w["w_cq"][l], BF16)
        kvm = _matmul(hm, w["w_ckv"][l], BF16)
        o = _xattn(q, kvm, mem_tokens, seq)
        x = _matmul(o, w["w_co"][l], F32, epilogue="residual", residual=x)

        h = _rmsnorm(x, w["norm_mlp"][l], BF16)
        a = _matmul(h, w["w_mlp_in"][l], BF16, epilogue="relu2")
        x = _matmul_kacc(a, w["w_mlp_out"][l], x)
    return _rmsnorm(x, w["norm_final"], F32).reshape(batch, seq, d)


def kernel(x_prompt, x_sample, mem_prompt, mem_sample, norm_mix, w_in, ret_decay_f, ret_decay_b, attn_sink,
           rel_bias, w_out, norm_cross, norm_mem, w_cq, w_ckv, w_co, norm_mlp, w_mlp_in, w_mlp_out, norm_final):
    w = dict(depth=w_in.shape[0], norm_mix=norm_mix, w_in=w_in.astype(BF16), dec_f=ret_decay_f,
             dec_b=ret_decay_b, sink=attn_sink, w_out=w_out.astype(BF16), norm_cross=norm_cross,
             norm_mem=norm_mem, w_cq=w_cq.astype(BF16), w_ckv=w_ckv.astype(BF16), w_co=w_co.astype(BF16),
             norm_mlp=norm_mlp, w_mlp_in=w_mlp_in.astype(BF16), w_mlp_out=w_mlp_out.astype(BF16),
             norm_final=norm_final)
    bias = _band_bias(rel_bias)
    return (_encode(x_prompt, mem_prompt, w, bias), _encode(x_sample, mem_sample, w, bias))
```

```python
import functools
import math

import numpy as np
import jax
import jax.numpy as jnp
from jax import lax
from jax.experimental import pallas as pl
from jax.experimental.pallas import tpu as pltpu

HEAD_DIM = 128
RET_CHUNK = 128
WINDOW = 128
BLOCK = 128
N_BUCKETS = 32
MAX_DISTANCE = 128
MEM_HEADS = 4
GQA_GROUP = 4
ROPE_BASE = 10000.0
RMS_EPS = 1e-6
NEG_INF = -1e30

VMEM_LIMIT_BYTES = 56 * 1024 * 1024

BF16 = jnp.bfloat16
F32 = jnp.float32

ROW_TILES = (1024, 512, 256, 128, 64, 32, 16, 8)
COL_TILES = (1024, 512, 256, 128)


def _pick(n, prefs):
    for p in prefs:
        if n % p == 0:
            return p
    raise ValueError(f"no tile in {prefs} divides {n}")


def _params(sem):
    return pltpu.CompilerParams(dimension_semantics=sem, vmem_limit_bytes=VMEM_LIMIT_BYTES)


def _rmsnorm_kernel(x_ref, g_ref, o_ref):
    x = x_ref[...]
    var = jnp.mean(x * x, axis=-1, keepdims=True)
    o_ref[...] = (x * lax.rsqrt(var + RMS_EPS) * g_ref[...]).astype(o_ref.dtype)


def _rmsnorm(x, g, out_dtype):
    n, d = x.shape
    tm = _pick(n, (256, 128, 64, 32, 16, 8))
    return pl.pallas_call(
        _rmsnorm_kernel,
        grid=(n // tm,),
        in_specs=[pl.BlockSpec((tm, d), lambda i: (i, 0)),
                  pl.BlockSpec((1, d), lambda i: (0, 0))],
        out_specs=pl.BlockSpec((tm, d), lambda i: (i, 0)),
        out_shape=jax.ShapeDtypeStruct((n, d), out_dtype),
        compiler_params=_params(("parallel",)),
        name="rmsnorm",
    )(x, g.reshape(1, d).astype(F32))


def _mm_kernel(a_ref, b_ref, o_ref, *, epilogue):
    acc = jnp.dot(a_ref[...], b_ref[...], preferred_element_type=F32)
    if epilogue == "relu2":
        acc = jnp.maximum(acc, 0.0)
        acc = acc * acc
    o_ref[...] = acc.astype(o_ref.dtype)


def _matmul(a, b, out_dtype, epilogue="none"):
    m, k = a.shape
    _, n = b.shape
    tm = _pick(m, ROW_TILES)
    tn = _pick(n, COL_TILES)
    return pl.pallas_call(
        functools.partial(_mm_kernel, epilogue=epilogue),
        grid=(m // tm, n // tn),
        in_specs=[pl.BlockSpec((tm, k), lambda i, j: (i, 0)),
                  pl.BlockSpec((k, tn), lambda i, j: (0, j))],
        out_specs=pl.BlockSpec((tm, tn), lambda i, j: (i, j)),
        out_shape=jax.ShapeDtypeStruct((m, n), out_dtype),
        compiler_params=_params(("parallel", "arbitrary")),
        name="matmul_" + epilogue,
    )(a, b)


def _inproj_kernel(a_ref, b_ref, cos_ref, sin_ref, o_ref, *, q_tiles, k_tiles):
    j = pl.program_id(1)

    def _dot():
        return jnp.dot(a_ref[...], b_ref[...], preferred_element_type=F32)

    def _store_rope(scale):
        acc = _dot()
        cos = cos_ref[...]
        sin = sin_ref[...]
        if scale != 1.0:
            cos = cos * scale
            sin = sin * scale
        for h in range(acc.shape[1] // HEAD_DIM):
            cols = slice(h * HEAD_DIM, (h + 1) * HEAD_DIM)
            x = acc[:, cols]
            o_ref[:, cols] = (x * cos + pltpu.roll(x, HEAD_DIM // 2, 1) * sin).astype(o_ref.dtype)

    @pl.when(j < q_tiles)
    def _():
        _store_rope(1.0)

    @pl.when(jnp.logical_and(j >= q_tiles, j < q_tiles + k_tiles))
    def _():
        _store_rope(HEAD_DIM ** -0.5)

    @pl.when(j >= q_tiles + k_tiles)
    def _():
        o_ref[...] = _dot().astype(o_ref.dtype)


def _inproj(h, w_in, cos_t, sin_t, ret_w, seq):
    m, k = h.shape
    _, n = w_in.shape
    tn = _pick(math.gcd(n, ret_w), COL_TILES)
    tm = _pick(seq, ROW_TILES)
    pos_spec = pl.BlockSpec((tm, HEAD_DIM), lambda i, j: (i % (seq // tm), 0))
    return pl.pallas_call(
        functools.partial(_inproj_kernel, q_tiles=ret_w // tn, k_tiles=ret_w // tn),
        grid=(m // tm, n // tn),
        in_specs=[pl.BlockSpec((tm, k), lambda i, j: (i, 0)),
                  pl.BlockSpec((k, tn), lambda i, j: (0, j)),
                  pos_spec, pos_spec],
        out_specs=pl.BlockSpec((tm, tn), lambda i, j: (i, j)),
        out_shape=jax.ShapeDtypeStruct((m, n), BF16),
        compiler_params=_params(("parallel", "arbitrary")),
        name="inproj_rope",
    )(h, w_in, cos_t, sin_t)


RET_MAX_UNROLL = 4


def _ret_kernel(decf_ref, decb_ref, q_ref, k_ref, v_ref, g_ref, o_ref, st_scr, tab_scr, *, nc):
    C = RET_CHUNK
    RET_UNROLL = math.gcd(nc, RET_MAX_UNROLL)
    lgf = -jnp.exp(jnp.broadcast_to(decf_ref[0, 0:1, :], (C, C)))
    lgb = -jnp.exp(jnp.broadcast_to(decb_ref[0, 0:1, :], (C, C)))
    row = lax.broadcasted_iota(jnp.int32, (C, C), 0).astype(F32)
    col = lax.broadcasted_iota(jnp.int32, (C, C), 1).astype(F32)
    diff = row - col
    tab_scr[0] = jnp.where(diff >= 0, jnp.exp(lgf * jnp.maximum(diff, 0.0)),
                           jnp.exp(lgb * jnp.maximum(-diff, 0.0)))
    tab_scr[1] = jnp.exp(lgf * (row + 1.0))
    tab_scr[2] = jnp.exp(lgb * (C - row))
    tab_scr[3] = jnp.exp(lgf * (C - 1.0 - row))
    tab_scr[4] = jnp.exp(lgb * row)
    cdf = jnp.exp(lgf * C)
    cdb = jnp.exp(lgb * C)

    tdims = (((0,), (0,)), ((), ()))
    ndims = (((1,), (1,)), ((), ()))

    def chunk_rows(c):
        return pl.ds(pl.multiple_of(c * C, C), C)

    def scan_body(t, carry):
        sf, sb = carry
        cf = t
        cb = nc - 1 - t
        st_scr[cf, 0:C, :] = sf.astype(BF16)
        st_scr[cb, C:2 * C, :] = sb.astype(BF16)
        rf = chunk_rows(cf)
        rb = chunk_rows(cb)
        kzf = (k_ref[rf, :].astype(F32) * tab_scr[3]).astype(BF16)
        kzb = (k_ref[rb, :].astype(F32) * tab_scr[4]).astype(BF16)
        sf = sf * cdf + lax.dot_general(kzf, v_ref[rf, :], tdims, preferred_element_type=F32)
        sb = sb * cdb + lax.dot_general(kzb, v_ref[rb, :], tdims, preferred_element_type=F32)
        return sf, sb

    zero = jnp.zeros((C, C), F32)
    lax.fori_loop(0, nc, scan_body, (zero, zero), unroll=RET_UNROLL)

    def out_body(t, carry):
        cs = [t * RET_UNROLL + i for i in range(RET_UNROLL)]
        rows = [chunk_rows(c) for c in cs]
        qs = [q_ref[r, :] for r in rows]
        ss = [lax.dot_general(q, k_ref[r, :], ndims, preferred_element_type=F32) for q, r in zip(qs, rows)]
        atts = [(s * tab_scr[0]).astype(BF16) for s in ss]
        qxs = []
        for q in qs:
            qf = q.astype(F32)
            qxs.append(jnp.concatenate([(qf * tab_scr[1]).astype(BF16), (qf * tab_scr[2]).astype(BF16)], axis=1))
        ys = [jnp.dot(att, v_ref[r, :], preferred_element_type=F32)
              + jnp.dot(qx, st_scr[c], preferred_element_type=F32)
              for att, qx, r, c in zip(atts, qxs, rows, cs)]
        for y, r in zip(ys, rows):
            var = jnp.mean(y * y, axis=-1, keepdims=True)
            g = g_ref[r, :].astype(F32)
            gate = g * (1.0 / (1.0 + jnp.exp(-g)))
            o_ref[r, :] = (y * lax.rsqrt(var + RMS_EPS) * gate).astype(o_ref.dtype)
        return carry

    lax.fori_loop(0, nc // RET_UNROLL, out_body, 0)


def _retention(proj, dec_f, dec_b, ret_heads, seq):
    n = proj.shape[0]
    nc = seq // RET_CHUNK
    dec_f = jnp.broadcast_to(dec_f.astype(F32).reshape(ret_heads, 1, 1), (ret_heads, 8, HEAD_DIM))
    dec_b = jnp.broadcast_to(dec_b.astype(F32).reshape(ret_heads, 1, 1), (ret_heads, 8, HEAD_DIM))
    dec_spec = pl.BlockSpec((1, 8, HEAD_DIM), lambda b, h: (h, 0, 0))

    def col_spec(base):
        return pl.BlockSpec((seq, HEAD_DIM), lambda b, h: (b, base + h))

    return pl.pallas_call(
        functools.partial(_ret_kernel, nc=nc),
        grid=(n // seq, ret_heads),
        in_specs=[dec_spec, dec_spec, col_spec(0), col_spec(ret_heads), col_spec(2 * ret_heads),
                  col_spec(3 * ret_heads)],
        out_specs=pl.BlockSpec((seq, HEAD_DIM), lambda b, h: (b, h)),
        out_shape=jax.ShapeDtypeStruct((n, ret_heads * HEAD_DIM), BF16),
        scratch_shapes=[pltpu.VMEM((nc, 2 * RET_CHUNK, RET_CHUNK), BF16),
                        pltpu.VMEM((5, RET_CHUNK, RET_CHUNK), F32)],
        compiler_params=_params(("parallel", "arbitrary")),
        name="retention",
    )(dec_f, dec_b, proj, proj, proj, proj)


SWA_QBLOCKS = 2


def _swa_kernel(sink_ref, q_ref, k_ref, v_ref, bias_ref, o_ref, *, nblk):
    B = BLOCK
    kvh = pl.program_id(1)
    step = pl.program_id(2)
    row = lax.broadcasted_iota(jnp.int32, (B, 3 * B), 0)
    col = lax.broadcasted_iota(jnp.int32, (B, 3 * B), 1)
    in_window = jnp.abs(col - B - row) <= WINDOW
    ndims = (((1,), (1,)), ((), ()))
    bands = []
    scores = []
    for qb in range(SWA_QBLOCKS):
        n = step * SWA_QBLOCKS + qb
        has_prev = n > 0
        has_next = n < nblk - 1
        rp = pl.ds(pl.multiple_of(jnp.maximum(n - 1, 0) * B, B), B)
        rc = pl.ds(pl.multiple_of(n * B, B), B)
        rn = pl.ds(pl.multiple_of(jnp.minimum(n + 1, nblk - 1) * B, B), B)
        kband = jnp.concatenate([k_ref[rp, :], k_ref[rc, :], k_ref[rn, :]], axis=0)
        vband = jnp.concatenate([v_ref[rp, :], v_ref[rc, :], v_ref[rn, :]], axis=0)
        ok = jnp.logical_and(in_window,
                             jnp.logical_and(jnp.logical_or(col >= B, has_prev),
                                             jnp.logical_or(col < 2 * B, has_next)))
        bands.append((vband, ok))
        for g in range(GQA_GROUP):
            q = q_ref[qb * B:(qb + 1) * B, g * HEAD_DIM:(g + 1) * HEAD_DIM]
            scores.append(lax.dot_general(q, kband, ndims, preferred_element_type=F32))
    probs = []
    for idx, s in enumerate(scores):
        qb, g = divmod(idx, GQA_GROUP)
        sink = sink_ref[kvh * GQA_GROUP + g]
        s = jnp.where(bands[qb][1], s * (HEAD_DIM ** -0.5) + bias_ref[g], NEG_INF)
        m = jnp.maximum(jnp.max(s, axis=-1, keepdims=True), sink)
        p = jnp.exp(s - m)
        denom = jnp.sum(p, axis=-1, keepdims=True) + jnp.exp(sink - m)
        probs.append((p.astype(BF16), 1.0 / denom))
    for idx, (p, inv) in enumerate(probs):
        qb, g = divmod(idx, GQA_GROUP)
        o = jnp.dot(p, bands[qb][0], preferred_element_type=F32) * inv
        o_ref[qb * B:(qb + 1) * B, g * HEAD_DIM:(g + 1) * HEAD_DIM] = o.astype(o_ref.dtype)


def _swa(proj, sink, bias, ret_w, att_heads, seq):
    n = proj.shape[0]
    kv_heads = att_heads // GQA_GROUP
    nblk = seq // BLOCK
    assert nblk % SWA_QBLOCKS == 0
    steps = nblk // SWA_QBLOCKS
    qrows = SWA_QBLOCKS * BLOCK
    gw = GQA_GROUP * HEAD_DIM
    q_base = 4 * ret_w // gw
    k_base = (4 * ret_w + att_heads * HEAD_DIM) // HEAD_DIM
    v_base = k_base + kv_heads
    return pl.pallas_call(
        functools.partial(_swa_kernel, nblk=nblk),
        grid=(n // seq, kv_heads, steps),
        in_specs=[pl.BlockSpec(memory_space=pltpu.SMEM),
                  pl.BlockSpec((qrows, gw), lambda b, kv, i: (b * steps + i, q_base + kv)),
                  pl.BlockSpec((seq, HEAD_DIM), lambda b, kv, i: (b, k_base + kv)),
                  pl.BlockSpec((seq, HEAD_DIM), lambda b, kv, i: (b, v_base + kv)),
                  pl.BlockSpec((GQA_GROUP, BLOCK, 3 * BLOCK), lambda b, kv, i: (kv, 0, 0))],
        out_specs=pl.BlockSpec((qrows, gw), lambda b, kv, i: (b * steps + i, kv)),
        out_shape=jax.ShapeDtypeStruct((n, att_heads * HEAD_DIM), BF16),
        compiler_params=_params(("parallel", "arbitrary", "arbitrary")),
        name="swa",
    )(sink.astype(F32), proj, proj, proj, bias)


def _t5_buckets(rel):
    nb = N_BUCKETS // 2
    max_exact = nb // 2
    base = np.where(rel > 0, nb, 0)
    n = np.abs(rel)
    large = max_exact + (np.log(np.maximum(n, 1) / max_exact) / np.log(MAX_DISTANCE / max_exact)
                         * (nb - max_exact)).astype(np.int32)
    large = np.minimum(large, nb - 1)
    return (base + np.where(n < max_exact, n, large)).astype(np.int32)


def _band_bias(rel_bias):
    qi = np.arange(BLOCK)[:, None]
    kj = np.arange(3 * BLOCK)[None, :] - BLOCK
    buckets = jnp.asarray(_t5_buckets(kj - qi).reshape(-1, 1))
    onehot = (buckets == jnp.arange(N_BUCKETS, dtype=jnp.int32)[None, :]).astype(F32)
    table = jnp.dot(onehot, rel_bias.astype(F32), precision=lax.Precision.HIGHEST)
    return jnp.transpose(table).reshape(rel_bias.shape[1], BLOCK, 3 * BLOCK)


def _cross_kernel(x_ref, gc_ref, gm_ref, wq_ref, kv_ref, wo_ref, x2_ref, h3_ref):
    ndims = (((1,), (1,)), ((), ()))
    mem_w = MEM_HEADS * HEAD_DIM
    x = x_ref[...]
    var = jnp.mean(x * x, axis=-1, keepdims=True)
    hn = (x * lax.rsqrt(var + RMS_EPS) * gc_ref[...]).astype(BF16)
    q = jnp.dot(hn, wq_ref[...], preferred_element_type=F32).astype(BF16)
    scores = [lax.dot_general(q[:, h * HEAD_DIM:(h + 1) * HEAD_DIM], kv_ref[:, h * HEAD_DIM:(h + 1) * HEAD_DIM],
                              ndims, preferred_element_type=F32) for h in range(MEM_HEADS)]
    probs = []
    for s in scores:
        s = s * (HEAD_DIM ** -0.5)
        p = jnp.exp(s - jnp.max(s, axis=-1, keepdims=True))
        probs.append((p.astype(BF16), 1.0 / jnp.sum(p, axis=-1, keepdims=True)))
    outs = []
    for h, (p, inv) in enumerate(probs):
        v = kv_ref[:, mem_w + h * HEAD_DIM:mem_w + (h + 1) * HEAD_DIM]
        outs.append((jnp.dot(p, v, preferred_element_type=F32) * inv).astype(BF16))
    o = jnp.concatenate(outs, axis=1)
    x2 = jnp.dot(o, wo_ref[...], preferred_element_type=F32) + x_ref[...]
    x2_ref[...] = x2
    var2 = jnp.mean(x2 * x2, axis=-1, keepdims=True)
    h3_ref[...] = (x2 * lax.rsqrt(var2 + RMS_EPS) * gm_ref[...]).astype(h3_ref.dtype)


def _cross_block(x, g_cross, g_mlp, w_cq, kvm, w_co, mem_tokens, seq):
    n, d = x.shape
    mem_w = w_cq.shape[1]
    tm = _pick(seq, (256, 128, 64, 32, 16, 8))
    row_spec = pl.BlockSpec((tm, d), lambda i: (i, 0))
    gain_spec = pl.BlockSpec((1, d), lambda i: (0, 0))
    return pl.pallas_call(
        _cross_kernel,
        grid=(n // tm,),
        in_specs=[row_spec, gain_spec, gain_spec,
                  pl.BlockSpec((d, mem_w), lambda i: (0, 0)),
                  pl.BlockSpec((mem_tokens, 2 * mem_w), lambda i: (i // (seq // tm), 0)),
                  pl.BlockSpec((mem_w, d), lambda i: (0, 0))],
        out_specs=[row_spec, row_spec],
        out_shape=[jax.ShapeDtypeStruct((n, d), F32), jax.ShapeDtypeStruct((n, d), BF16)],
        compiler_params=_params(("parallel",)),
        name="cross_block",
    )(x, g_cross.reshape(1, d).astype(F32), g_mlp.reshape(1, d).astype(F32), w_cq, kvm, w_co)


def _outproj_kernel(a1_ref, a2_ref, b1_ref, b2_ref, r_ref, o_ref):
    acc = jnp.dot(a1_ref[...], b1_ref[...], preferred_element_type=F32)
    acc = acc + jnp.dot(a2_ref[...], b2_ref[...], preferred_element_type=F32)
    o_ref[...] = acc + r_ref[...]


def _outproj(y_ret, y_att, w_out, x):
    m, k1 = y_ret.shape
    _, k2 = y_att.shape
    assert k1 % k2 == 0
    n = w_out.shape[1]
    tm = _pick(m, ROW_TILES)
    tn = _pick(n, COL_TILES)
    return pl.pallas_call(
        _outproj_kernel,
        grid=(m // tm, n // tn),
        in_specs=[pl.BlockSpec((tm, k1), lambda i, j: (i, 0)),
                  pl.BlockSpec((tm, k2), lambda i, j: (i, 0)),
                  pl.BlockSpec((k1, tn), lambda i, j: (0, j)),
                  pl.BlockSpec((k2, tn), lambda i, j: (k1 // k2, j)),
                  pl.BlockSpec((tm, tn), lambda i, j: (i, j))],
        out_specs=pl.BlockSpec((tm, tn), lambda i, j: (i, j)),
        out_shape=jax.ShapeDtypeStruct((m, n), F32),
        compiler_params=_params(("parallel", "arbitrary")),
        name="outproj",
    )(y_ret, y_att, w_out, w_out, x)


def _mm_kacc_kernel(a_ref, b_ref, r_ref, o_ref):
    kk = pl.program_id(2)

    @pl.when(kk == 0)
    def _():
        o_ref[...] = r_ref[...] + jnp.dot(a_ref[...], b_ref[...], preferred_element_type=F32)

    @pl.when(kk != 0)
    def _():
        o_ref[...] += jnp.dot(a_ref[...], b_ref[...], preferred_element_type=F32)


def _matmul_kacc(a, b, residual):
    m, k = a.shape
    _, n = b.shape
    tm = _pick(m, ROW_TILES)
    tn = _pick(n, (2048,) + COL_TILES)
    tk = _pick(k, COL_TILES)
    return pl.pallas_call(
        _mm_kacc_kernel,
        grid=(m // tm, n // tn, k // tk),
        in_specs=[pl.BlockSpec((tm, tk), lambda i, j, kk: (i, kk)),
                  pl.BlockSpec((tk, tn), lambda i, j, kk: (kk, j)),
                  pl.BlockSpec((tm, tn), lambda i, j, kk: (i, j))],
        out_specs=pl.BlockSpec((tm, tn), lambda i, j, kk: (i, j)),
        out_shape=jax.ShapeDtypeStruct((m, n), F32),
        compiler_params=_params(("parallel", "arbitrary", "arbitrary")),
        name="matmul_kacc",
    )(a, b, residual)


def _rope_tables(seq):
    half = HEAD_DIM // 2
    inv = ROPE_BASE ** (-jnp.arange(half, dtype=F32) / half)
    ang = jnp.arange(seq, dtype=F32)[:, None] * inv[None, :]
    cos = jnp.cos(ang)
    sin = jnp.sin(ang)
    return jnp.concatenate([cos, cos], axis=-1), jnp.concatenate([-sin, sin], axis=-1)


def _encode(x3d, mem3d, w, bias):
    batch, seq, d = x3d.shape
    mem_tokens = mem3d.shape[1]
    x = x3d.reshape(batch * seq, d)
    mem = mem3d.reshape(batch * mem_tokens, d)
    cos_t, sin_t = _rope_tables(seq)
    for l in range(w["depth"]):
        ret_heads = w["dec_f"][l].shape[0]
        att_heads = w["sink"][l].shape[0]
        ret_w = ret_heads * HEAD_DIM
        h = _rmsnorm(x, w["norm_mix"][l], BF16)
        proj = _inproj(h, w["w_in"][l], cos_t, sin_t, ret_w, seq)
        y_ret = _retention(proj, w["dec_f"][l], w["dec_b"][l], ret_heads, seq)
        y_att = _swa(proj, w["sink"][l], bias, ret_w, att_heads, seq)
        x = _outproj(y_ret, y_att, w["w_out"][l], x)

        hm = _rmsnorm(mem, w["norm_mem"][l], BF16)
        kvm = _matmul(hm, w["w_ckv"][l], BF16)
        x, h = _cross_block(x, w["norm_cross"][l], w["norm_mlp"][l], w["w_cq"][l], kvm, w["w_co"][l],
                            mem_tokens, seq)

        a = _matmul(h, w["w_mlp_in"][l], BF16, epilogue="relu2")
        x = _matmul_kacc(a, w["w_mlp_out"][l], x)
    return _rmsnorm(x, w["norm_final"], F32).reshape(batch, seq, d)


def kernel(x_prompt, x_sample, mem_prompt, mem_sample, norm_mix, w_in, ret_decay_f, ret_decay_b, attn_sink,
           rel_bias, w_out, norm_cross, norm_mem, w_cq, w_ckv, w_co, norm_mlp, w_mlp_in, w_mlp_out, norm_final):
    w = dict(depth=w_in.shape[0], norm_mix=norm_mix, w_in=w_in.astype(BF16), dec_f=ret_decay_f,
             dec_b=ret_decay_b, sink=attn_sink, w_out=w_out.astype(BF16), norm_cross=norm_cross,
             norm_mem=norm_mem, w_cq=w_cq.astype(BF16), w_ckv=w_ckv.astype(BF16), w_co=w_co.astype(BF16),
             norm_mlp=norm_mlp, w_mlp_in=w_mlp_in.astype(BF16), w_mlp_out=w_mlp_out.astype(BF16),
             norm_final=norm_final)
    bias = _band_bias(rel_bias)
    return (_encode(x_prompt, mem_prompt, w, bias), _encode(x_sample, mem_sample, w, bias))
```

```python
import functools
import math

import numpy as np
import jax
import jax.numpy as jnp
from jax import lax
from jax.experimental import pallas as pl
from jax.experimental.pallas import tpu as pltpu

HEAD_DIM = 128
RET_CHUNK = 128
WINDOW = 128
BLOCK = 128
N_BUCKETS = 32
MAX_DISTANCE = 128
MEM_HEADS = 4
GQA_GROUP = 4
ROPE_BASE = 10000.0
RMS_EPS = 1e-6
NEG_INF = -1e30

VMEM_LIMIT_BYTES = 56 * 1024 * 1024

BF16 = jnp.bfloat16
F32 = jnp.float32

ROW_TILES = (1024, 512, 256, 128, 64, 32, 16, 8)
COL_TILES = (1024, 512, 256, 128)


def _pick(n, prefs):
    for p in prefs:
        if n % p == 0:
            return p
    raise ValueError(f"no tile in {prefs} divides {n}")


def _params(sem):
    return pltpu.CompilerParams(dimension_semantics=sem, vmem_limit_bytes=VMEM_LIMIT_BYTES)


def _rmsnorm_kernel(x_ref, g_ref, o_ref):
    x = x_ref[...]
    var = jnp.mean(x * x, axis=-1, keepdims=True)
    o_ref[...] = (x * lax.rsqrt(var + RMS_EPS) * g_ref[...]).astype(o_ref.dtype)


def _rmsnorm(x, g, out_dtype):
    n, d = x.shape
    tm = _pick(n, (256, 128, 64, 32, 16, 8))
    return pl.pallas_call(
        _rmsnorm_kernel,
        grid=(n // tm,),
        in_specs=[pl.BlockSpec((tm, d), lambda i: (i, 0)),
                  pl.BlockSpec((1, d), lambda i: (0, 0))],
        out_specs=pl.BlockSpec((tm, d), lambda i: (i, 0)),
        out_shape=jax.ShapeDtypeStruct((n, d), out_dtype),
        compiler_params=_params(("parallel",)),
        name="rmsnorm",
    )(x, g.reshape(1, d).astype(F32))


def _mm_kernel(a_ref, b_ref, o_ref, *, epilogue):
    acc = jnp.dot(a_ref[...], b_ref[...], preferred_element_type=F32)
    if epilogue == "relu2":
        acc = jnp.maximum(acc, 0.0)
        acc = acc * acc
    o_ref[...] = acc.astype(o_ref.dtype)


def _matmul(a, b, out_dtype, epilogue="none"):
    m, k = a.shape
    _, n = b.shape
    tm = _pick(m, ROW_TILES)
    tn = _pick(n, COL_TILES)
    return pl.pallas_call(
        functools.partial(_mm_kernel, epilogue=epilogue),
        grid=(m // tm, n // tn),
        in_specs=[pl.BlockSpec((tm, k), lambda i, j: (i, 0)),
                  pl.BlockSpec((k, tn), lambda i, j: (0, j))],
        out_specs=pl.BlockSpec((tm, tn), lambda i, j: (i, j)),
        out_shape=jax.ShapeDtypeStruct((m, n), out_dtype),
        compiler_params=_params(("parallel", "arbitrary")),
        name="matmul_" + epilogue,
    )(a, b)


def _inproj_kernel(a_ref, b_ref, cos_ref, sin_ref, o_ref, *, q_tiles, k_tiles):
    j = pl.program_id(1)

    def _dot():
        return jnp.dot(a_ref[...], b_ref[...], preferred_element_type=F32)

    def _store_rope(scale):
        acc = _dot()
        cos = cos_ref[...]
        sin = sin_ref[...]
        if scale != 1.0:
            cos = cos * scale
            sin = sin * scale
        for h in range(acc.shape[1] // HEAD_DIM):
            cols = slice(h * HEAD_DIM, (h + 1) * HEAD_DIM)
            x = acc[:, cols]
            o_ref[:, cols] = (x * cos + pltpu.roll(x, HEAD_DIM // 2, 1) * sin).astype(o_ref.dtype)

    @pl.when(j < q_tiles)
    def _():
        _store_rope(1.0)

    @pl.when(jnp.logical_and(j >= q_tiles, j < q_tiles + k_tiles))
    def _():
        _store_rope(HEAD_DIM ** -0.5)

    @pl.when(j >= q_tiles + k_tiles)
    def _():
        o_ref[...] = _dot().astype(o_ref.dtype)


def _inproj(h, w_in, cos_t, sin_t, ret_w, seq):
    m, k = h.shape
    _, n = w_in.shape
    tn = _pick(math.gcd(n, ret_w), COL_TILES)
    tm = _pick(seq, ROW_TILES)
    pos_spec = pl.BlockSpec((tm, HEAD_DIM), lambda i, j: (i % (seq // tm), 0))
    return pl.pallas_call(
        functools.partial(_inproj_kernel, q_tiles=ret_w // tn, k_tiles=ret_w // tn),
        grid=(m // tm, n // tn),
        in_specs=[pl.BlockSpec((tm, k), lambda i, j: (i, 0)),
                  pl.BlockSpec((k, tn), lambda i, j: (0, j)),
                  pos_spec, pos_spec],
        out_specs=pl.BlockSpec((tm, tn), lambda i, j: (i, j)),
        out_shape=jax.ShapeDtypeStruct((m, n), BF16),
        compiler_params=_params(("parallel", "arbitrary")),
        name="inproj_rope",
    )(h, w_in, cos_t, sin_t)


RET_MAX_UNROLL = 8


def _ret_kernel(decf_ref, decb_ref, q_ref, k_ref, v_ref, g_ref, o_ref, st_scr, tab_scr, *, nc):
    C = RET_CHUNK
    RET_UNROLL = math.gcd(nc, RET_MAX_UNROLL)
    lgf = -jnp.exp(jnp.broadcast_to(decf_ref[0, 0:1, :], (C, C)))
    lgb = -jnp.exp(jnp.broadcast_to(decb_ref[0, 0:1, :], (C, C)))
    row = lax.broadcasted_iota(jnp.int32, (C, C), 0).astype(F32)
    col = lax.broadcasted_iota(jnp.int32, (C, C), 1).astype(F32)
    diff = row - col
    tab_scr[0] = jnp.where(diff >= 0, jnp.exp(lgf * jnp.maximum(diff, 0.0)),
                           jnp.exp(lgb * jnp.maximum(-diff, 0.0)))
    tab_scr[1] = jnp.exp(lgf * (row + 1.0))
    tab_scr[2] = jnp.exp(lgb * (C - row))
    tab_scr[3] = jnp.exp(lgf * (C - 1.0 - row))
    tab_scr[4] = jnp.exp(lgb * row)
    cdf = jnp.exp(lgf * C)
    cdb = jnp.exp(lgb * C)

    tdims = (((0,), (0,)), ((), ()))
    ndims = (((1,), (1,)), ((), ()))

    def chunk_rows(c):
        return pl.ds(pl.multiple_of(c * C, C), C)

    def scan_body(t, carry):
        sf, sb = carry
        cf = t
        cb = nc - 1 - t
        st_scr[cf, 0:C, :] = sf.astype(BF16)
        st_scr[cb, C:2 * C, :] = sb.astype(BF16)
        rf = chunk_rows(cf)
        rb = chunk_rows(cb)
        kzf = (k_ref[rf, :].astype(F32) * tab_scr[3]).astype(BF16)
        kzb = (k_ref[rb, :].astype(F32) * tab_scr[4]).astype(BF16)
        sf = sf * cdf + lax.dot_general(kzf, v_ref[rf, :], tdims, preferred_element_type=F32)
        sb = sb * cdb + lax.dot_general(kzb, v_ref[rb, :], tdims, preferred_element_type=F32)
        return sf, sb

    zero = jnp.zeros((C, C), F32)
    lax.fori_loop(0, nc, scan_body, (zero, zero), unroll=RET_UNROLL)

    def out_body(t, carry):
        cs = [t * RET_UNROLL + i for i in range(RET_UNROLL)]
        rows = [chunk_rows(c) for c in cs]
        qs = [q_ref[r, :] for r in rows]
        ss = [lax.dot_general(q, k_ref[r, :], ndims, preferred_element_type=F32) for q, r in zip(qs, rows)]
        atts = [(s * tab_scr[0]).astype(BF16) for s in ss]
        qxs = []
        for q in qs:
            qf = q.astype(F32)
            qxs.append(jnp.concatenate([(qf * tab_scr[1]).astype(BF16), (qf * tab_scr[2]).astype(BF16)], axis=1))
        ys = [jnp.dot(att, v_ref[r, :], preferred_element_type=F32)
              + jnp.dot(qx, st_scr[c], preferred_element_type=F32)
              for att, qx, r, c in zip(atts, qxs, rows, cs)]
        for y, r in zip(ys, rows):
            var = jnp.mean(y * y, axis=-1, keepdims=True)
            g = g_ref[r, :].astype(F32)
            gate = g * (1.0 / (1.0 + jnp.exp(-g)))
            o_ref[r, :] = (y * lax.rsqrt(var + RMS_EPS) * gate).astype(o_ref.dtype)
        return carry

    lax.fori_loop(0, nc // RET_UNROLL, out_body, 0)


def _retention(proj, dec_f, dec_b, ret_heads, seq):
    n = proj.shape[0]
    nc = seq // RET_CHUNK
    dec_f = jnp.broadcast_to(dec_f.astype(F32).reshape(ret_heads, 1, 1), (ret_heads, 8, HEAD_DIM))
    dec_b = jnp.broadcast_to(dec_b.astype(F32).reshape(ret_heads, 1, 1), (ret_heads, 8, HEAD_DIM))
    dec_spec = pl.BlockSpec((1, 8, HEAD_DIM), lambda b, h: (h, 0, 0))

    def col_spec(base):
        return pl.BlockSpec((seq, HEAD_DIM), lambda b, h: (b, base + h))

    return pl.pallas_call(
        functools.partial(_ret_kernel, nc=nc),
        grid=(n // seq, ret_heads),
        in_specs=[dec_spec, dec_spec, col_spec(0), col_spec(ret_heads), col_spec(2 * ret_heads),
                  col_spec(3 * ret_heads)],
        out_specs=pl.BlockSpec((seq, HEAD_DIM), lambda b, h: (b, h)),
        out_shape=jax.ShapeDtypeStruct((n, ret_heads * HEAD_DIM), BF16),
        scratch_shapes=[pltpu.VMEM((nc, 2 * RET_CHUNK, RET_CHUNK), BF16),
                        pltpu.VMEM((5, RET_CHUNK, RET_CHUNK), F32)],
        compiler_params=_params(("parallel", "arbitrary")),
        name="retention",
    )(dec_f, dec_b, proj, proj, proj, proj)


SWA_MAX_QBLOCKS = 4


def _swa_kernel(sink_ref, q_ref, k_ref, v_ref, bias_ref, o_ref, *, nblk):
    B = BLOCK
    SWA_QBLOCKS = q_ref.shape[0] // B
    kvh = pl.program_id(1)
    step = pl.program_id(2)
    ndims = (((1,), (1,)), ((), ()))
    scale = HEAD_DIM ** -0.5
    exp2_coef = scale * math.log2(math.e)
    bands = []
    scores = []
    for qb in range(SWA_QBLOCKS):
        n = step * SWA_QBLOCKS + qb
        variant = jnp.where(n > 0, 0, 1) + jnp.where(n < nblk - 1, 0, 2)
        rp = pl.ds(pl.multiple_of(jnp.maximum(n - 1, 0) * B, B), B)
        rc = pl.ds(pl.multiple_of(n * B, B), B)
        rn = pl.ds(pl.multiple_of(jnp.minimum(n + 1, nblk - 1) * B, B), B)
        kband = jnp.concatenate([k_ref[rp, :], k_ref[rc, :], k_ref[rn, :]], axis=0)
        vband = jnp.concatenate([v_ref[rp, :], v_ref[rc, :], v_ref[rn, :]], axis=0)
        bands.append((vband, variant))
        for g in range(GQA_GROUP):
            q = q_ref[qb * B:(qb + 1) * B, g * HEAD_DIM:(g + 1) * HEAD_DIM]
            scores.append(lax.dot_general(q, kband, ndims, preferred_element_type=F32))
    probs = []
    for idx, s in enumerate(scores):
        qb, g = divmod(idx, GQA_GROUP)
        sink = sink_ref[kvh * GQA_GROUP + g] * (1.0 / scale)
        t = s + bias_ref[bands[qb][1], g]
        mx = jnp.maximum(jnp.max(t, axis=-1, keepdims=True), sink)
        p = jnp.exp2((t - mx) * exp2_coef)
        denom = jnp.sum(p, axis=-1, keepdims=True) + jnp.exp2((sink - mx) * exp2_coef)
        probs.append((p.astype(BF16), 1.0 / denom))
    for idx, (p, inv) in enumerate(probs):
        qb, g = divmod(idx, GQA_GROUP)
        o = jnp.dot(p, bands[qb][0], preferred_element_type=F32) * inv
        o_ref[qb * B:(qb + 1) * B, g * HEAD_DIM:(g + 1) * HEAD_DIM] = o.astype(o_ref.dtype)


def _swa(proj, sink, bias, ret_w, att_heads, seq):
    n = proj.shape[0]
    kv_heads = att_heads // GQA_GROUP
    nblk = seq // BLOCK
    qblocks = math.gcd(nblk, SWA_MAX_QBLOCKS)
    steps = nblk // qblocks
    qrows = qblocks * BLOCK
    gw = GQA_GROUP * HEAD_DIM
    q_base = 4 * ret_w // gw
    k_base = (4 * ret_w + att_heads * HEAD_DIM) // HEAD_DIM
    v_base = k_base + kv_heads
    return pl.pallas_call(
        functools.partial(_swa_kernel, nblk=nblk),
        grid=(n // seq, kv_heads, steps),
        in_specs=[pl.BlockSpec(memory_space=pltpu.SMEM),
                  pl.BlockSpec((qrows, gw), lambda b, kv, i: (b * steps + i, q_base + kv)),
                  pl.BlockSpec((seq, HEAD_DIM), lambda b, kv, i: (b, k_base + kv)),
                  pl.BlockSpec((seq, HEAD_DIM), lambda b, kv, i: (b, v_base + kv)),
                  pl.BlockSpec((4, GQA_GROUP, BLOCK, 3 * BLOCK), lambda b, kv, i: (0, kv, 0, 0))],
        out_specs=pl.BlockSpec((qrows, gw), lambda b, kv, i: (b * steps + i, kv)),
        out_shape=jax.ShapeDtypeStruct((n, att_heads * HEAD_DIM), BF16),
        compiler_params=_params(("parallel", "arbitrary", "arbitrary")),
        name="swa",
    )(sink.astype(F32), proj, proj, proj, bias)


def _t5_buckets(rel):
    nb = N_BUCKETS // 2
    max_exact = nb // 2
    base = np.where(rel > 0, nb, 0)
    n = np.abs(rel)
    large = max_exact + (np.log(np.maximum(n, 1) / max_exact) / np.log(MAX_DISTANCE / max_exact)
                         * (nb - max_exact)).astype(np.int32)
    large = np.minimum(large, nb - 1)
    return (base + np.where(n < max_exact, n, large)).astype(np.int32)


def _band_bias(rel_bias):
    qi = np.arange(BLOCK)[:, None]
    kj = np.arange(3 * BLOCK)[None, :] - BLOCK
    rel = kj - qi
    buckets = jnp.asarray(_t5_buckets(rel).reshape(-1, 1))
    onehot = (buckets == jnp.arange(N_BUCKETS, dtype=jnp.int32)[None, :]).astype(F32)
    table = jnp.dot(onehot, rel_bias.astype(F32), precision=lax.Precision.HIGHEST)
    table = jnp.transpose(table).reshape(rel_bias.shape[1], BLOCK, 3 * BLOCK) * (HEAD_DIM ** 0.5)
    in_window = np.abs(rel) <= WINDOW
    variants = []
    for v in range(4):
        ok = in_window & ((kj >= 0) | (v & 1 == 0)) & ((kj < BLOCK) | (v & 2 == 0))
        variants.append(jnp.where(jnp.asarray(ok)[None], table, NEG_INF))
    return jnp.stack(variants)


def _cross_kernel(x_ref, gc_ref, gm_ref, wq_ref, kv_ref, wo_ref, x2_ref, h3_ref):
    ndims = (((1,), (1,)), ((), ()))
    mem_w = MEM_HEADS * HEAD_DIM
    x = x_ref[...]
    var = jnp.mean(x * x, axis=-1, keepdims=True)
    hn = (x * lax.rsqrt(var + RMS_EPS) * gc_ref[...]).astype(BF16)
    q = jnp.dot(hn, wq_ref[...], preferred_element_type=F32).astype(BF16)
    scores = [lax.dot_general(q[:, h * HEAD_DIM:(h + 1) * HEAD_DIM], kv_ref[:, h * HEAD_DIM:(h + 1) * HEAD_DIM],
                              ndims, preferred_element_type=F32) for h in range(MEM_HEADS)]
    probs = []
    for s in scores:
        s = s * (HEAD_DIM ** -0.5)
        p = jnp.exp(s - jnp.max(s, axis=-1, keepdims=True))
        probs.append((p.astype(BF16), 1.0 / jnp.sum(p, axis=-1, keepdims=True)))
    outs = []
    for h, (p, inv) in enumerate(probs):
        v = kv_ref[:, mem_w + h * HEAD_DIM:mem_w + (h + 1) * HEAD_DIM]
        outs.append((jnp.dot(p, v, preferred_element_type=F32) * inv).astype(BF16))
    o = jnp.concatenate(outs, axis=1)
    x2 = jnp.dot(o, wo_ref[...], preferred_element_type=F32) + x_ref[...]
    x2_ref[...] = x2
    var2 = jnp.mean(x2 * x2, axis=-1, keepdims=True)
    h3_ref[...] = (x2 * lax.rsqrt(var2 + RMS_EPS) * gm_ref[...]).astype(h3_ref.dtype)


def _cross_block(x, g_cross, g_mlp, w_cq, kvm, w_co, mem_tokens, seq):
    n, d = x.shape
    mem_w = w_cq.shape[1]
    tm = _pick(seq, (256, 128, 64, 32, 16, 8))
    row_spec = pl.BlockSpec((tm, d), lambda i: (i, 0))
    gain_spec = pl.BlockSpec((1, d), lambda i: (0, 0))
    return pl.pallas_call(
        _cross_kernel,
        grid=(n // tm,),
        in_specs=[row_spec, gain_spec, gain_spec,
                  pl.BlockSpec((d, mem_w), lambda i: (0, 0)),
                  pl.BlockSpec((mem_tokens, 2 * mem_w), lambda i: (i // (seq // tm), 0)),
                  pl.BlockSpec((mem_w, d), lambda i: (0, 0))],
        out_specs=[row_spec, row_spec],
        out_shape=[jax.ShapeDtypeStruct((n, d), F32), jax.ShapeDtypeStruct((n, d), BF16)],
        compiler_params=_params(("parallel",)),
        name="cross_block",
    )(x, g_cross.reshape(1, d).astype(F32), g_mlp.reshape(1, d).astype(F32), w_cq, kvm, w_co)


def _outproj_kernel(a1_ref, a2_ref, b1_ref, b2_ref, r_ref, o_ref):
    acc = jnp.dot(a1_ref[...], b1_ref[...], preferred_element_type=F32)
    acc = acc + jnp.dot(a2_ref[...], b2_ref[...], preferred_element_type=F32)
    o_ref[...] = acc + r_ref[...]


def _outproj(y_ret, y_att, w_out, x):
    m, k1 = y_ret.shape
    _, k2 = y_att.shape
    assert k1 % k2 == 0
    n = w_out.shape[1]
    tm = _pick(m, ROW_TILES)
    tn = _pick(n, COL_TILES)
    return pl.pallas_call(
        _outproj_kernel,
        grid=(m // tm, n // tn),
        in_specs=[pl.BlockSpec((tm, k1), lambda i, j: (i, 0)),
                  pl.BlockSpec((tm, k2), lambda i, j: (i, 0)),
                  pl.BlockSpec((k1, tn), lambda i, j: (0, j)),
                  pl.BlockSpec((k2, tn), lambda i, j: (k1 // k2, j)),
                  pl.BlockSpec((tm, tn), lambda i, j: (i, j))],
        out_specs=pl.BlockSpec((tm, tn), lambda i, j: (i, j)),
        out_shape=jax.ShapeDtypeStruct((m, n), F32),
        compiler_params=_params(("parallel", "arbitrary")),
        name="outproj",
    )(y_ret, y_att, w_out, w_out, x)


def _mm_kacc_kernel(a_ref, b_ref, r_ref, o_ref):
    kk = pl.program_id(2)

    @pl.when(kk == 0)
    def _():
        o_ref[...] = r_ref[...] + jnp.dot(a_ref[...], b_ref[...], preferred_element_type=F32)

    @pl.when(kk != 0)
    def _():
        o_ref[...] += jnp.dot(a_ref[...], b_ref[...], preferred_element_type=F32)


def _matmul_kacc(a, b, residual):
    m, k = a.shape
    _, n = b.shape
    tm = _pick(m, ROW_TILES)
    tn = _pick(n, COL_TILES)
    tk = _pick(k, (4096, 2048) + COL_TILES)
    return pl.pallas_call(
        _mm_kacc_kernel,
        grid=(m // tm, n // tn, k // tk),
        in_specs=[pl.BlockSpec((tm, tk), lambda i, j, kk: (i, kk)),
                  pl.BlockSpec((tk, tn), lambda i, j, kk: (kk, j)),
                  pl.BlockSpec((tm, tn), lambda i, j, kk: (i, j))],
        out_specs=pl.BlockSpec((tm, tn), lambda i, j, kk: (i, j)),
        out_shape=jax.ShapeDtypeStruct((m, n), F32),
        compiler_params=_params(("parallel", "arbitrary", "arbitrary")),
        name="matmul_kacc",
    )(a, b, residual)


def _rope_tables(seq):
    half = HEAD_DIM // 2
    inv = ROPE_BASE ** (-jnp.arange(half, dtype=F32) / half)
    ang = jnp.arange(seq, dtype=F32)[:, None] * inv[None, :]
    cos = jnp.cos(ang)
    sin = jnp.sin(ang)
    return jnp.concatenate([cos, cos], axis=-1), jnp.concatenate([-sin, sin], axis=-1)


def _encode(x3d, mem3d, w, bias):
    batch, seq, d = x3d.shape
    mem_tokens = mem3d.shape[1]
    x = x3d.reshape(batch * seq, d)
    mem = mem3d.reshape(batch * mem_tokens, d)
    cos_t, sin_t = _rope_tables(seq)
    for l in range(w["depth"]):
        ret_heads = w["dec_f"][l].shape[0]
        att_heads = w["sink"][l].shape[0]
        ret_w = ret_heads * HEAD_DIM
        h = _rmsnorm(x, w["norm_mix"][l], BF16)
        proj = _inproj(h, w["w_in"][l], cos_t, sin_t, ret_w, seq)
        y_ret = _retention(proj, w["dec_f"][l], w["dec_b"][l], ret_heads, seq)
        y_att = _swa(proj, w["sink"][l], bias, ret_w, att_heads, seq)
        x = _outproj(y_ret, y_att, w["w_out"][l], x)

        hm = _rmsnorm(mem, w["norm_mem"][l], BF16)
        kvm = _matmul(hm, w["w_ckv"][l], BF16)
        x, h = _cross_block(x, w["norm_cross"][l], w["norm_mlp"][l], w["w_cq"][l], kvm, w["w_co"][l],
                            mem_tokens, seq)

        a = _matmul(h, w["w_mlp_in"][l], BF16, epilogue="relu2")
        x = _matmul_kacc(a, w["w_mlp_out"][l], x)
    return _rmsnorm(x, w["norm_final"], F32).reshape(batch, seq, d)


def kernel(x_prompt, x_sample, mem_prompt, mem_sample, norm_mix, w_in, ret_decay_f, ret_decay_b, attn_sink,
           rel_bias, w_out, norm_cross, norm_mem, w_cq, w_ckv, w_co, norm_mlp, w_mlp_in, w_mlp_out, norm_final):
    w = dict(depth=w_in.shape[0], norm_mix=norm_mix, w_in=w_in.astype(BF16), dec_f=ret_decay_f,
             dec_b=ret_decay_b, sink=attn_sink, w_out=w_out.astype(BF16), norm_cross=norm_cross,
             norm_mem=norm_mem, w_cq=w_cq.astype(BF16), w_ckv=w_ckv.astype(BF16), w_co=w_co.astype(BF16),
             norm_mlp=norm_mlp, w_mlp_in=w_mlp_in.astype(BF16), w_mlp_out=w_mlp_out.astype(BF16),
             norm_final=norm_final)
    bias = _band_bias(rel_bias)
    return (_encode(x_prompt, mem_prompt, w, bias), _encode(x_sample, mem_sample, w, bias))
```

```python
import functools
import math

import numpy as np
import jax
import jax.numpy as jnp
from jax import lax
from jax.experimental import pallas as pl
from jax.experimental.pallas import tpu as pltpu

HEAD_DIM = 128
RET_CHUNK = 128
WINDOW = 128
BLOCK = 128
N_BUCKETS = 32
MAX_DISTANCE = 128
MEM_HEADS = 4
GQA_GROUP = 4
ROPE_BASE = 10000.0
RMS_EPS = 1e-6
NEG_INF = -1e30

VMEM_LIMIT_BYTES = 60 * 1024 * 1024

BF16 = jnp.bfloat16
F32 = jnp.float32

ROW_TILES = (1024, 512, 256, 128, 64, 32, 16, 8)
COL_TILES = (1024, 512, 256, 128)


def _pick(n, prefs):
    for p in prefs:
        if n % p == 0:
            return p
    raise ValueError(f"no tile in {prefs} divides {n}")


def _params(sem):
    return pltpu.CompilerParams(dimension_semantics=sem, vmem_limit_bytes=VMEM_LIMIT_BYTES)


def _rmsnorm_kernel(x_ref, g_ref, o_ref):
    x = x_ref[...]
    var = jnp.mean(x * x, axis=-1, keepdims=True)
    o_ref[...] = (x * lax.rsqrt(var + RMS_EPS) * g_ref[...]).astype(o_ref.dtype)


def _rmsnorm(x, g, out_dtype):
    n, d = x.shape
    tm = _pick(n, (256, 128, 64, 32, 16, 8))
    return pl.pallas_call(
        _rmsnorm_kernel,
        grid=(n // tm,),
        in_specs=[pl.BlockSpec((tm, d), lambda i: (i, 0)),
                  pl.BlockSpec((1, d), lambda i: (0, 0))],
        out_specs=pl.BlockSpec((tm, d), lambda i: (i, 0)),
        out_shape=jax.ShapeDtypeStruct((n, d), out_dtype),
        compiler_params=_params(("parallel",)),
        name="rmsnorm",
    )(x, g.reshape(1, d).astype(F32))


class _SideJob:
    def __init__(self, src, gain=None, out_dtype=BF16):
        self.src = src
        self.gain = gain
        self.out_dtype = out_dtype

    def plan(self, nblocks, block_of_step):
        rows, cols = self.src.shape
        if rows % nblocks or (rows // nblocks) % 16:
            return None
        spec = pl.BlockSpec((rows // nblocks, cols), lambda *ids: (block_of_step(*ids), 0))
        in_specs, args = [spec], [self.src]
        if self.gain is not None:
            in_specs.append(pl.BlockSpec((1, cols), lambda *ids: (0, 0)))
            args.append(self.gain.reshape(1, cols).astype(F32))
        return in_specs, args, spec, jax.ShapeDtypeStruct(self.src.shape, self.out_dtype)

    def standalone(self):
        if self.gain is None:
            return self.src.astype(self.out_dtype)
        return _rmsnorm(self.src, self.gain, self.out_dtype)


def _side_body(in_refs, out_ref):
    if len(in_refs) == 1:
        out_ref[...] = in_refs[0][...].astype(out_ref.dtype)
    else:
        _rmsnorm_kernel(in_refs[0], in_refs[1], out_ref)


def _split_side(rest, n_main_in, n_main_out):
    n_side_in = len(rest) - n_main_in - n_main_out
    if n_side_in == 0:
        return rest[:n_main_in], rest[n_main_in:], None
    n_side_in -= 1
    main_in = rest[:n_main_in]
    side_in = rest[n_main_in:n_main_in + n_side_in]
    main_out = rest[n_main_in + n_side_in:n_main_in + n_side_in + n_main_out]
    return main_in, main_out, (side_in, rest[-1])


def _call_with_side(kernel_fn, name, grid, sem, in_specs, args, out_spec, out_shape, side, nblocks,
                    block_of_step):
    plan = side.plan(nblocks, block_of_step) if side is not None else None
    if plan is None:
        out = pl.pallas_call(kernel_fn, grid=grid, in_specs=in_specs, out_specs=out_spec,
                             out_shape=out_shape, compiler_params=_params(sem), name=name)(*args)
        return out, (side.standalone() if side is not None else None)
    s_in_specs, s_args, s_out_spec, s_out_shape = plan
    out, side_out = pl.pallas_call(
        kernel_fn, grid=grid, in_specs=in_specs + s_in_specs, out_specs=[out_spec, s_out_spec],
        out_shape=[out_shape, s_out_shape], compiler_params=_params(sem), name=name + "_side",
    )(*args, *s_args)
    return out, side_out


def _mm_kernel(*refs, epilogue):
    (a_ref, b_ref), (o_ref,), side = _split_side(refs, 2, 1)
    acc = jnp.dot(a_ref[...], b_ref[...], preferred_element_type=F32)
    if epilogue == "relu2":
        acc = jnp.maximum(acc, 0.0)
        acc = acc * acc
    o_ref[...] = acc.astype(o_ref.dtype)
    if side is not None:
        _side_body(*side)


def _matmul(a, b, out_dtype, epilogue="none", side=None):
    m, k = a.shape
    _, n = b.shape
    tm = _pick(m, ROW_TILES)
    tn = _pick(n, COL_TILES)
    nj = n // tn
    return _call_with_side(
        functools.partial(_mm_kernel, epilogue=epilogue), "matmul_" + epilogue,
        (m // tm, nj), ("parallel", "arbitrary"),
        [pl.BlockSpec((tm, k), lambda i, j: (i, 0)), pl.BlockSpec((k, tn), lambda i, j: (0, j))],
        [a, b], pl.BlockSpec((tm, tn), lambda i, j: (i, j)), jax.ShapeDtypeStruct((m, n), out_dtype),
        side, (m // tm) * nj, lambda i, j: i * nj + j)


def _inproj_kernel(*refs, q_tiles, k_tiles):
    (a_ref, b_ref, cos_ref, sin_ref), (o_ref,), side = _split_side(refs, 4, 1)
    j = pl.program_id(1)

    def _dot():
        if side is not None:
            _side_body(*side)
        return jnp.dot(a_ref[...], b_ref[...], preferred_element_type=F32)

    def _store_rope(scale):
        acc = _dot()
        cos = cos_ref[...]
        sin = sin_ref[...]
        if scale != 1.0:
            cos = cos * scale
            sin = sin * scale
        for h in range(acc.shape[1] // HEAD_DIM):
            cols = slice(h * HEAD_DIM, (h + 1) * HEAD_DIM)
            x = acc[:, cols]
            o_ref[:, cols] = (x * cos + pltpu.roll(x, HEAD_DIM // 2, 1) * sin).astype(o_ref.dtype)

    @pl.when(j < q_tiles)
    def _():
        _store_rope(1.0)

    @pl.when(jnp.logical_and(j >= q_tiles, j < q_tiles + k_tiles))
    def _():
        _store_rope(HEAD_DIM ** -0.5)

    @pl.when(j >= q_tiles + k_tiles)
    def _():
        o_ref[...] = _dot().astype(o_ref.dtype)


def _inproj(h, w_in, cos_t, sin_t, ret_w, seq, side=None):
    m, k = h.shape
    _, n = w_in.shape
    tn = _pick(math.gcd(n, ret_w), COL_TILES)
    tm = _pick(seq, ROW_TILES)
    nj = n // tn
    nj_side = 1 << (nj.bit_length() - 1)
    pos_spec = pl.BlockSpec((tm, HEAD_DIM), lambda i, j: (i % (seq // tm), 0))
    return _call_with_side(
        functools.partial(_inproj_kernel, q_tiles=ret_w // tn, k_tiles=ret_w // tn), "inproj_rope",
        (m // tm, nj), ("parallel", "arbitrary"),
        [pl.BlockSpec((tm, k), lambda i, j: (i, 0)), pl.BlockSpec((k, tn), lambda i, j: (0, j)),
         pos_spec, pos_spec],
        [h, w_in, cos_t, sin_t], pl.BlockSpec((tm, tn), lambda i, j: (i, j)),
        jax.ShapeDtypeStruct((m, n), BF16),
        side, (m // tm) * nj_side, lambda i, j: i * nj_side + jnp.minimum(j, nj_side - 1))


RET_MAX_UNROLL = 8


def _ret_kernel(decf_ref, decb_ref, q_ref, k_ref, v_ref, g_ref, o_ref, st_scr, tab_scr, *, nc):
    C = RET_CHUNK
    RET_UNROLL = math.gcd(nc, RET_MAX_UNROLL)
    lgf = -jnp.exp(jnp.broadcast_to(decf_ref[0, 0:1, :], (C, C)))
    lgb = -jnp.exp(jnp.broadcast_to(decb_ref[0, 0:1, :], (C, C)))
    row = lax.broadcasted_iota(jnp.int32, (C, C), 0).astype(F32)
    col = lax.broadcasted_iota(jnp.int32, (C, C), 1).astype(F32)
    diff = row - col
    tab_scr[0] = jnp.where(diff >= 0, jnp.exp(lgf * jnp.maximum(diff, 0.0)),
                           jnp.exp(lgb * jnp.maximum(-diff, 0.0)))
    tab_scr[1] = jnp.exp(lgf * (row + 1.0))
    tab_scr[2] = jnp.exp(lgb * (C - row))
    tab_scr[3] = jnp.exp(lgf * (C - 1.0 - row))
    tab_scr[4] = jnp.exp(lgb * row)
    cdf = jnp.exp(lgf * C)
    cdb = jnp.exp(lgb * C)

    tdims = (((0,), (0,)), ((), ()))
    ndims = (((1,), (1,)), ((), ()))

    def chunk_rows(c):
        return pl.ds(pl.multiple_of(c * C, C), C)

    def scan_body(t, carry):
        sf, sb = carry
        cf = t
        cb = nc - 1 - t
        st_scr[cf, 0:C, :] = sf.astype(BF16)
        st_scr[cb, C:2 * C, :] = sb.astype(BF16)
        rf = chunk_rows(cf)
        rb = chunk_rows(cb)
        kzf = (k_ref[rf, :].astype(F32) * tab_scr[3]).astype(BF16)
        kzb = (k_ref[rb, :].astype(F32) * tab_scr[4]).astype(BF16)
        sf = sf * cdf + lax.dot_general(kzf, v_ref[rf, :], tdims, preferred_element_type=F32)
        sb = sb * cdb + lax.dot_general(kzb, v_ref[rb, :], tdims, preferred_element_type=F32)
        return sf, sb

    zero = jnp.zeros((C, C), F32)
    lax.fori_loop(0, nc, scan_body, (zero, zero), unroll=RET_UNROLL)

    def out_body(t, carry):
        cs = [t * RET_UNROLL + i for i in range(RET_UNROLL)]
        rows = [chunk_rows(c) for c in cs]
        qs = [q_ref[r, :] for r in rows]
        ss = [lax.dot_general(q, k_ref[r, :], ndims, preferred_element_type=F32) for q, r in zip(qs, rows)]
        atts = [(s * tab_scr[0]).astype(BF16) for s in ss]
        qxs = []
        for q in qs:
            qf = q.astype(F32)
            qxs.append(jnp.concatenate([(qf * tab_scr[1]).astype(BF16), (qf * tab_scr[2]).astype(BF16)], axis=1))
        ys = [jnp.dot(att, v_ref[r, :], preferred_element_type=F32)
              + jnp.dot(qx, st_scr[c], preferred_element_type=F32)
              for att, qx, r, c in zip(atts, qxs, rows, cs)]
        for y, r in zip(ys, rows):
            var = jnp.mean(y * y, axis=-1, keepdims=True)
            g = g_ref[r, :].astype(F32)
            gate = g * (1.0 / (1.0 + jnp.exp(-g)))
            o_ref[r, :] = (y * lax.rsqrt(var + RMS_EPS) * gate).astype(o_ref.dtype)
        return carry

    lax.fori_loop(0, nc // RET_UNROLL, out_body, 0)


def _retention(proj, dec_f, dec_b, ret_heads, seq):
    n = proj.shape[0]
    nc = seq // RET_CHUNK
    dec_f = jnp.broadcast_to(dec_f.astype(F32).reshape(ret_heads, 1, 1), (ret_heads, 8, HEAD_DIM))
    dec_b = jnp.broadcast_to(dec_b.astype(F32).reshape(ret_heads, 1, 1), (ret_heads, 8, HEAD_DIM))
    dec_spec = pl.BlockSpec((1, 8, HEAD_DIM), lambda b, h: (h, 0, 0))

    def col_spec(base):
        return pl.BlockSpec((seq, HEAD_DIM), lambda b, h: (b, base + h))

    return pl.pallas_call(
        functools.partial(_ret_kernel, nc=nc),
        grid=(n // seq, ret_heads),
        in_specs=[dec_spec, dec_spec, col_spec(0), col_spec(ret_heads), col_spec(2 * ret_heads),
                  col_spec(3 * ret_heads)],
        out_specs=pl.BlockSpec((seq, HEAD_DIM), lambda b, h: (b, h)),
        out_shape=jax.ShapeDtypeStruct((n, ret_heads * HEAD_DIM), BF16),
        scratch_shapes=[pltpu.VMEM((nc, 2 * RET_CHUNK, RET_CHUNK), BF16),
                        pltpu.VMEM((5, RET_CHUNK, RET_CHUNK), F32)],
        compiler_params=_params(("parallel", "arbitrary")),
        name="retention",
    )(dec_f, dec_b, proj, proj, proj, proj)


SWA_MAX_QBLOCKS = 4


def _swa_kernel(sink_ref, q_ref, k_ref, v_ref, bias_ref, o_ref, *, nblk):
    B = BLOCK
    SWA_QBLOCKS = q_ref.shape[0] // B
    kvh = pl.program_id(1)
    step = pl.program_id(2)
    ndims = (((1,), (1,)), ((), ()))
    scale = HEAD_DIM ** -0.5
    exp2_coef = scale * math.log2(math.e)
    bands = []
    scores = []
    for qb in range(SWA_QBLOCKS):
        n = step * SWA_QBLOCKS + qb
        variant = jnp.where(n > 0, 0, 1) + jnp.where(n < nblk - 1, 0, 2)
        rp = pl.ds(pl.multiple_of(jnp.maximum(n - 1, 0) * B, B), B)
        rc = pl.ds(pl.multiple_of(n * B, B), B)
        rn = pl.ds(pl.multiple_of(jnp.minimum(n + 1, nblk - 1) * B, B), B)
        kband = jnp.concatenate([k_ref[rp, :], k_ref[rc, :], k_ref[rn, :]], axis=0)
        vband = jnp.concatenate([v_ref[rp, :], v_ref[rc, :], v_ref[rn, :]], axis=0)
        bands.append((vband, variant))
        for g in range(GQA_GROUP):
            q = q_ref[qb * B:(qb + 1) * B, g * HEAD_DIM:(g + 1) * HEAD_DIM]
            scores.append(lax.dot_general(q, kband, ndims, preferred_element_type=F32))
    probs = []
    for idx, s in enumerate(scores):
        qb, g = divmod(idx, GQA_GROUP)
        sink = sink_ref[kvh * GQA_GROUP + g] * (1.0 / scale)
        t = s + bias_ref[bands[qb][1], g]
        mx = jnp.maximum(jnp.max(t, axis=-1, keepdims=True), sink)
        p = jnp.exp2((t - mx) * exp2_coef)
        denom = jnp.sum(p, axis=-1, keepdims=True) + jnp.exp2((sink - mx) * exp2_coef)
        probs.append((p.astype(BF16), 1.0 / denom))
    for idx, (p, inv) in enumerate(probs):
        qb, g = divmod(idx, GQA_GROUP)
        o = jnp.dot(p, bands[qb][0], preferred_element_type=F32) * inv
        o_ref[qb * B:(qb + 1) * B, g * HEAD_DIM:(g + 1) * HEAD_DIM] = o.astype(o_ref.dtype)


def _swa(proj, sink, bias, ret_w, att_heads, seq):
    n = proj.shape[0]
    kv_heads = att_heads // GQA_GROUP
    nblk = seq // BLOCK
    qblocks = math.gcd(nblk, SWA_MAX_QBLOCKS)
    steps = nblk // qblocks
    qrows = qblocks * BLOCK
    gw = GQA_GROUP * HEAD_DIM
    q_base = 4 * ret_w // gw
    k_base = (4 * ret_w + att_heads * HEAD_DIM) // HEAD_DIM
    v_base = k_base + kv_heads
    return pl.pallas_call(
        functools.partial(_swa_kernel, nblk=nblk),
        grid=(n // seq, kv_heads, steps),
        in_specs=[pl.BlockSpec(memory_space=pltpu.SMEM),
                  pl.BlockSpec((qrows, gw), lambda b, kv, i: (b * steps + i, q_base + kv)),
                  pl.BlockSpec((seq, HEAD_DIM), lambda b, kv, i: (b, k_base + kv)),
                  pl.BlockSpec((seq, HEAD_DIM), lambda b, kv, i: (b, v_base + kv)),
                  pl.BlockSpec((4, GQA_GROUP, BLOCK, 3 * BLOCK), lambda b, kv, i: (0, kv, 0, 0))],
        out_specs=pl.BlockSpec((qrows, gw), lambda b, kv, i: (b * steps + i, kv)),
        out_shape=jax.ShapeDtypeStruct((n, att_heads * HEAD_DIM), BF16),
        compiler_params=_params(("parallel", "arbitrary", "arbitrary")),
        name="swa",
    )(sink.astype(F32), proj, proj, proj, bias)


def _t5_buckets(rel):
    nb = N_BUCKETS // 2
    max_exact = nb // 2
    base = np.where(rel > 0, nb, 0)
    n = np.abs(rel)
    large = max_exact + (np.log(np.maximum(n, 1) / max_exact) / np.log(MAX_DISTANCE / max_exact)
                         * (nb - max_exact)).astype(np.int32)
    large = np.minimum(large, nb - 1)
    return (base + np.where(n < max_exact, n, large)).astype(np.int32)


def _band_bias(rel_bias):
    qi = np.arange(BLOCK)[:, None]
    kj = np.arange(3 * BLOCK)[None, :] - BLOCK
    rel = kj - qi
    buckets = jnp.asarray(_t5_buckets(rel).reshape(-1, 1))
    onehot = (buckets == jnp.arange(N_BUCKETS, dtype=jnp.int32)[None, :]).astype(F32)
    table = jnp.dot(onehot, rel_bias.astype(F32), precision=lax.Precision.HIGHEST)
    table = jnp.transpose(table).reshape(rel_bias.shape[1], BLOCK, 3 * BLOCK) * (HEAD_DIM ** 0.5)
    in_window = np.abs(rel) <= WINDOW
    variants = []
    for v in range(4):
        ok = in_window & ((kj >= 0) | (v & 1 == 0)) & ((kj < BLOCK) | (v & 2 == 0))
        variants.append(jnp.where(jnp.asarray(ok)[None], table, NEG_INF))
    return jnp.stack(variants)


def _cross_kernel(x_ref, gc_ref, gm_ref, wq_ref, kv_ref, wo_ref, x2_ref, h3_ref):
    ndims = (((1,), (1,)), ((), ()))
    mem_w = MEM_HEADS * HEAD_DIM
    x = x_ref[...]
    var = jnp.mean(x * x, axis=-1, keepdims=True)
    hn = (x * lax.rsqrt(var + RMS_EPS) * gc_ref[...]).astype(BF16)
    q = jnp.dot(hn, wq_ref[...], preferred_element_type=F32).astype(BF16)
    scores = [lax.dot_general(q[:, h * HEAD_DIM:(h + 1) * HEAD_DIM], kv_ref[:, h * HEAD_DIM:(h + 1) * HEAD_DIM],
                              ndims, preferred_element_type=F32) for h in range(MEM_HEADS)]
    probs = []
    for s in scores:
        s = s * (HEAD_DIM ** -0.5)
        p = jnp.exp(s - jnp.max(s, axis=-1, keepdims=True))
        probs.append((p.astype(BF16), 1.0 / jnp.sum(p, axis=-1, keepdims=True)))
    outs = []
    for h, (p, inv) in enumerate(probs):
        v = kv_ref[:, mem_w + h * HEAD_DIM:mem_w + (h + 1) * HEAD_DIM]
        outs.append((jnp.dot(p, v, preferred_element_type=F32) * inv).astype(BF16))
    o = jnp.concatenate(outs, axis=1)
    x2 = jnp.dot(o, wo_ref[...], preferred_element_type=F32) + x_ref[...]
    x2_ref[...] = x2
    var2 = jnp.mean(x2 * x2, axis=-1, keepdims=True)
    h3_ref[...] = (x2 * lax.rsqrt(var2 + RMS_EPS) * gm_ref[...]).astype(h3_ref.dtype)


def _cross_block(x, g_cross, g_mlp, w_cq, kvm, w_co, mem_tokens, seq):
    n, d = x.shape
    mem_w = w_cq.shape[1]
    tm = _pick(seq, (256, 128, 64, 32, 16, 8))
    row_spec = pl.BlockSpec((tm, d), lambda i: (i, 0))
    gain_spec = pl.BlockSpec((1, d), lambda i: (0, 0))
    return pl.pallas_call(
        _cross_kernel,
        grid=(n // tm,),
        in_specs=[row_spec, gain_spec, gain_spec,
                  pl.BlockSpec((d, mem_w), lambda i: (0, 0)),
                  pl.BlockSpec((mem_tokens, 2 * mem_w), lambda i: (i // (seq // tm), 0)),
                  pl.BlockSpec((mem_w, d), lambda i: (0, 0))],
        out_specs=[row_spec, row_spec],
        out_shape=[jax.ShapeDtypeStruct((n, d), F32), jax.ShapeDtypeStruct((n, d), BF16)],
        compiler_params=_params(("parallel",)),
        name="cross_block",
    )(x, g_cross.reshape(1, d).astype(F32), g_mlp.reshape(1, d).astype(F32), w_cq, kvm, w_co)


def _outproj_kernel(a1_ref, a2_ref, b1_ref, b2_ref, r_ref, o_ref):
    acc = jnp.dot(a1_ref[...], b1_ref[...], preferred_element_type=F32)
    acc = acc + jnp.dot(a2_ref[...], b2_ref[...], preferred_element_type=F32)
    o_ref[...] = acc + r_ref[...]


def _outproj(y_ret, y_att, w_out, x):
    m, k1 = y_ret.shape
    _, k2 = y_att.shape
    assert k1 % k2 == 0
    n = w_out.shape[1]
    tm = _pick(m, ROW_TILES)
    tn = _pick(n, COL_TILES)
    return pl.pallas_call(
        _outproj_kernel,
        grid=(m // tm, n // tn),
        in_specs=[pl.BlockSpec((tm, k1), lambda i, j: (i, 0)),
                  pl.BlockSpec((tm, k2), lambda i, j: (i, 0)),
                  pl.BlockSpec((k1, tn), lambda i, j: (0, j)),
                  pl.BlockSpec((k2, tn), lambda i, j: (k1 // k2, j)),
                  pl.BlockSpec((tm, tn), lambda i, j: (i, j))],
        out_specs=pl.BlockSpec((tm, tn), lambda i, j: (i, j)),
        out_shape=jax.ShapeDtypeStruct((m, n), F32),
        compiler_params=_params(("parallel", "arbitrary")),
        name="outproj",
    )(y_ret, y_att, w_out, w_out, x)


def _mm_kacc_kernel(*refs):
    (a_ref, b_ref, r_ref), (o_ref,), side = _split_side(refs, 3, 1)
    kk = pl.program_id(2)

    def _dot():
        if side is not None:
            _side_body(*side)
        return jnp.dot(a_ref[...], b_ref[...], preferred_element_type=F32)

    @pl.when(kk == 0)
    def _():
        o_ref[...] = r_ref[...] + _dot()

    @pl.when(kk != 0)
    def _():
        o_ref[...] += _dot()


def _matmul_kacc(a, b, residual, side=None):
    m, k = a.shape
    _, n = b.shape
    tm = _pick(m, ROW_TILES)
    tn = _pick(n, COL_TILES)
    tk = _pick(k, (4096, 2048) + COL_TILES)
    nj, nk = n // tn, k // tk
    return _call_with_side(
        _mm_kacc_kernel, "matmul_kacc", (m // tm, nj, nk), ("parallel", "arbitrary", "arbitrary"),
        [pl.BlockSpec((tm, tk), lambda i, j, kk: (i, kk)), pl.BlockSpec((tk, tn), lambda i, j, kk: (kk, j)),
         pl.BlockSpec((tm, tn), lambda i, j, kk: (i, j))],
        [a, b, residual], pl.BlockSpec((tm, tn), lambda i, j, kk: (i, j)),
        jax.ShapeDtypeStruct((m, n), F32),
        side, (m // tm) * nj * nk, lambda i, j, kk: (i * nj + j) * nk + kk)


def _rope_tables(seq):
    half = HEAD_DIM // 2
    inv = ROPE_BASE ** (-jnp.arange(half, dtype=F32) / half)
    ang = jnp.arange(seq, dtype=F32)[:, None] * inv[None, :]
    cos = jnp.cos(ang)
    sin = jnp.sin(ang)
    return jnp.concatenate([cos, cos], axis=-1), jnp.concatenate([-sin, sin], axis=-1)


def _layer(x, h, mem, w, l, bias, rope, seq, mem_tokens, mlp_weights, side_mlp_in=None, side_mlp_out=None):
    ret_heads = w["dec_f"][l].shape[0]
    att_heads = w["sink"][l].shape[0]
    ret_w = ret_heads * HEAD_DIM
    if mlp_weights is None:
        assert side_mlp_in is None
        proj, w1 = _inproj(h, w["w_in"][l], *rope, ret_w, seq, side=_SideJob(w["w_mlp_in"][l]))
        side_mlp_in = _SideJob(w["w_mlp_out"][l])
    else:
        proj, _ = _inproj(h, w["w_in"][l], *rope, ret_w, seq)
        w1 = mlp_weights[0]
    y_ret = _retention(proj, w["dec_f"][l], w["dec_b"][l], ret_heads, seq)
    y_att = _swa(proj, w["sink"][l], bias, ret_w, att_heads, seq)
    x = _outproj(y_ret, y_att, w["w_out"][l], x)

    hm = _rmsnorm(mem, w["norm_mem"][l], BF16)
    kvm, _ = _matmul(hm, w["w_ckv"][l], BF16)
    x, h = _cross_block(x, w["norm_cross"][l], w["norm_mlp"][l], w["w_cq"][l], kvm, w["w_co"][l],
                        mem_tokens, seq)

    a, side1 = _matmul(h, w1, BF16, epilogue="relu2", side=side_mlp_in)
    if mlp_weights is None:
        mlp_weights, side1 = (w1, side1), None
    x, side2 = _matmul_kacc(a, mlp_weights[1], x, side=side_mlp_out)
    return x, mlp_weights, side1, side2


def kernel(x_prompt, x_sample, mem_prompt, mem_sample, norm_mix, w_in, ret_decay_f, ret_decay_b, attn_sink,
           rel_bias, w_out, norm_cross, norm_mem, w_cq, w_ckv, w_co, norm_mlp, w_mlp_in, w_mlp_out, norm_final):
    depth = w_in.shape[0]
    w = dict(norm_mix=norm_mix, w_in=w_in.astype(BF16), dec_f=ret_decay_f, dec_b=ret_decay_b, sink=attn_sink,
             w_out=w_out.astype(BF16), norm_cross=norm_cross, norm_mem=norm_mem, w_cq=w_cq.astype(BF16),
             w_ckv=w_ckv.astype(BF16), w_co=w_co.astype(BF16), norm_mlp=norm_mlp, w_mlp_in=w_mlp_in,
             w_mlp_out=w_mlp_out)
    bias = _band_bias(rel_bias)
    d = x_prompt.shape[-1]
    mem_tokens = mem_prompt.shape[1]
    seq_a, seq_b = x_prompt.shape[1], x_sample.shape[1]
    xa, xb = x_prompt.reshape(-1, d), x_sample.reshape(-1, d)
    ma, mb = mem_prompt.reshape(-1, d), mem_sample.reshape(-1, d)
    rope_a, rope_b = _rope_tables(seq_a), _rope_tables(seq_b)

    mlp_w = []
    hb = None
    for l in range(depth):
        ha = _rmsnorm(xa, norm_mix[l], BF16)
        side = _SideJob(xb, norm_mix[0], BF16) if l == 0 else None
        xa, wl, _, side_out = _layer(xa, ha, ma, w, l, bias, rope_a, seq_a, mem_tokens, None, side_mlp_out=side)
        mlp_w.append(wl)
        if l == 0:
            hb = side_out
    ya = None
    for l in range(depth):
        if l > 0:
            hb = _rmsnorm(xb, norm_mix[l], BF16)
        side = _SideJob(xa, norm_final, F32) if l == 0 else None
        xb, _, side_out, _ = _layer(xb, hb, mb, w, l, bias, rope_b, seq_b, mem_tokens, mlp_w[l], side_mlp_in=side)
        if l == 0:
            ya = side_out
    yb = _rmsnorm(xb, norm_final, F32)
    return ya.reshape(x_prompt.shape), yb.reshape(x_sample.shape)
```

```python
import functools
import math

import numpy as np
import jax
import jax.numpy as jnp
from jax import lax
from jax.experimental import pallas as pl
from jax.experimental.pallas import tpu as pltpu

HEAD_DIM = 128
RET_CHUNK = 128
WINDOW = 128
BLOCK = 128
N_BUCKETS = 32
MAX_DISTANCE = 128
MEM_HEADS = 4
GQA_GROUP = 4
ROPE_BASE = 10000.0
RMS_EPS = 1e-6
NEG_INF = -1e30

VMEM_LIMIT_BYTES = 60 * 1024 * 1024

BF16 = jnp.bfloat16
F32 = jnp.float32

ROW_TILES = (1024, 512, 256, 128, 64, 32, 16, 8)
COL_TILES = (1024, 512, 256, 128)


def _pick(n, prefs):
    for p in prefs:
        if n % p == 0:
            return p
    raise ValueError(f"no tile in {prefs} divides {n}")


def _params(sem):
    return pltpu.CompilerParams(dimension_semantics=sem, vmem_limit_bytes=VMEM_LIMIT_BYTES)


def _rmsnorm_kernel(x_ref, g_ref, o_ref):
    x = x_ref[...]
    var = jnp.mean(x * x, axis=-1, keepdims=True)
    o_ref[...] = (x * lax.rsqrt(var + RMS_EPS) * g_ref[...]).astype(o_ref.dtype)


def _rmsnorm(x, g, out_dtype):
    n, d = x.shape
    tm = _pick(n, (256, 128, 64, 32, 16, 8))
    return pl.pallas_call(
        _rmsnorm_kernel,
        grid=(n // tm,),
        in_specs=[pl.BlockSpec((tm, d), lambda i: (i, 0)),
                  pl.BlockSpec((1, d), lambda i: (0, 0))],
        out_specs=pl.BlockSpec((tm, d), lambda i: (i, 0)),
        out_shape=jax.ShapeDtypeStruct((n, d), out_dtype),
        compiler_params=_params(("parallel",)),
        name="rmsnorm",
    )(x, g.reshape(1, d).astype(F32))


class _SideJob:
    def __init__(self, src, gain=None, out_dtype=BF16):
        self.src = src
        self.gain = gain
        self.out_dtype = out_dtype

    def plan(self, nblocks, block_of_step):
        rows, cols = self.src.shape
        if rows % nblocks or (rows // nblocks) % 16:
            return None
        spec = pl.BlockSpec((rows // nblocks, cols), lambda *ids: (block_of_step(*ids), 0))
        in_specs, args = [spec], [self.src]
        if self.gain is not None:
            in_specs.append(pl.BlockSpec((1, cols), lambda *ids: (0, 0)))
            args.append(self.gain.reshape(1, cols).astype(F32))
        return in_specs, args, spec, jax.ShapeDtypeStruct(self.src.shape, self.out_dtype)

    def standalone(self):
        if self.gain is None:
            return self.src.astype(self.out_dtype)
        return _rmsnorm(self.src, self.gain, self.out_dtype)


def _side_body(in_refs, out_ref):
    if len(in_refs) == 1:
        out_ref[...] = in_refs[0][...].astype(out_ref.dtype)
    else:
        _rmsnorm_kernel(in_refs[0], in_refs[1], out_ref)


def _split_side(rest, n_main_in, n_main_out):
    n_side_in = len(rest) - n_main_in - n_main_out
    if n_side_in == 0:
        return rest[:n_main_in], rest[n_main_in:], None
    n_side_in -= 1
    main_in = rest[:n_main_in]
    side_in = rest[n_main_in:n_main_in + n_side_in]
    main_out = rest[n_main_in + n_side_in:n_main_in + n_side_in + n_main_out]
    return main_in, main_out, (side_in, rest[-1])


def _call_with_side(kernel_fn, name, grid, sem, in_specs, args, out_spec, out_shape, side, nblocks,
                    block_of_step):
    plan = side.plan(nblocks, block_of_step) if side is not None else None
    if plan is None:
        out = pl.pallas_call(kernel_fn, grid=grid, in_specs=in_specs, out_specs=out_spec,
                             out_shape=out_shape, compiler_params=_params(sem), name=name)(*args)
        return out, (side.standalone() if side is not None else None)
    s_in_specs, s_args, s_out_spec, s_out_shape = plan
    out, side_out = pl.pallas_call(
        kernel_fn, grid=grid, in_specs=in_specs + s_in_specs, out_specs=[out_spec, s_out_spec],
        out_shape=[out_shape, s_out_shape], compiler_params=_params(sem), name=name + "_side",
    )(*args, *s_args)
    return out, side_out


def _mm_kernel(*refs, epilogue):
    (a_ref, b_ref), (o_ref,), side = _split_side(refs, 2, 1)
    acc = jnp.dot(a_ref[...], b_ref[...], preferred_element_type=F32)
    if epilogue == "relu2":
        acc = jnp.maximum(acc, 0.0)
        acc = acc * acc
    o_ref[...] = acc.astype(o_ref.dtype)
    if side is not None:
        _side_body(*side)


def _matmul(a, b, out_dtype, epilogue="none", side=None):
    m, k = a.shape
    _, n = b.shape
    tm = _pick(m, ROW_TILES)
    tn = _pick(n, COL_TILES)
    nj = n // tn
    return _call_with_side(
        functools.partial(_mm_kernel, epilogue=epilogue), "matmul_" + epilogue,
        (m // tm, nj), ("parallel", "arbitrary"),
        [pl.BlockSpec((tm, k), lambda i, j: (i, 0)), pl.BlockSpec((k, tn), lambda i, j: (0, j))],
        [a, b], pl.BlockSpec((tm, tn), lambda i, j: (i, j)), jax.ShapeDtypeStruct((m, n), out_dtype),
        side, (m // tm) * nj, lambda i, j: i * nj + j)


def _inproj_kernel(*refs, q_tiles, k_tiles):
    (a_ref, b_ref, cos_ref, sin_ref), (o_ref,), side = _split_side(refs, 4, 1)
    j = pl.program_id(1)

    def _dot():
        if side is not None:
            _side_body(*side)
        return jnp.dot(a_ref[...], b_ref[...], preferred_element_type=F32)

    def _store_rope(scale):
        acc = _dot()
        cos = cos_ref[...]
        sin = sin_ref[...]
        if scale != 1.0:
            cos = cos * scale
            sin = sin * scale
        for h in range(acc.shape[1] // HEAD_DIM):
            cols = slice(h * HEAD_DIM, (h + 1) * HEAD_DIM)
            x = acc[:, cols]
            o_ref[:, cols] = (x * cos + pltpu.roll(x, HEAD_DIM // 2, 1) * sin).astype(o_ref.dtype)

    @pl.when(j < q_tiles)
    def _():
        _store_rope(1.0)

    @pl.when(jnp.logical_and(j >= q_tiles, j < q_tiles + k_tiles))
    def _():
        _store_rope(HEAD_DIM ** -0.5)

    @pl.when(j >= q_tiles + k_tiles)
    def _():
        o_ref[...] = _dot().astype(o_ref.dtype)


def _inproj(h, w_in, cos_t, sin_t, ret_w, seq, side=None):
    m, k = h.shape
    _, n = w_in.shape
    tn = _pick(math.gcd(n, ret_w), COL_TILES)
    tm = _pick(seq, ROW_TILES)
    nj = n // tn
    nj_side = 1 << (nj.bit_length() - 1)
    pos_spec = pl.BlockSpec((tm, HEAD_DIM), lambda i, j: (i % (seq // tm), 0))
    return _call_with_side(
        functools.partial(_inproj_kernel, q_tiles=ret_w // tn, k_tiles=ret_w // tn), "inproj_rope",
        (m // tm, nj), ("parallel", "arbitrary"),
        [pl.BlockSpec((tm, k), lambda i, j: (i, 0)), pl.BlockSpec((k, tn), lambda i, j: (0, j)),
         pos_spec, pos_spec],
        [h, w_in, cos_t, sin_t], pl.BlockSpec((tm, tn), lambda i, j: (i, j)),
        jax.ShapeDtypeStruct((m, n), BF16),
        side, (m // tm) * nj_side, lambda i, j: i * nj_side + jnp.minimum(j, nj_side - 1))


RET_MAX_UNROLL = 16


def _ret_kernel(decf_ref, decb_ref, q_ref, k_ref, v_ref, g_ref, o_ref, st_scr, tab_scr, *, nc):
    C = RET_CHUNK
    RET_UNROLL = math.gcd(nc, RET_MAX_UNROLL)
    lgf = -jnp.exp(jnp.broadcast_to(decf_ref[0, 0:1, :], (C, C)))
    lgb = -jnp.exp(jnp.broadcast_to(decb_ref[0, 0:1, :], (C, C)))
    row = lax.broadcasted_iota(jnp.int32, (C, C), 0).astype(F32)
    col = lax.broadcasted_iota(jnp.int32, (C, C), 1).astype(F32)
    diff = row - col
    tab_scr[0] = jnp.where(diff >= 0, jnp.exp(lgf * jnp.maximum(diff, 0.0)),
                           jnp.exp(lgb * jnp.maximum(-diff, 0.0)))
    tab_scr[1] = jnp.exp(lgf * (row + 1.0))
    tab_scr[2] = jnp.exp(lgb * (C - row))
    tab_scr[3] = jnp.exp(lgf * (C - 1.0 - row))
    tab_scr[4] = jnp.exp(lgb * row)
    cdf = jnp.exp(lgf * C)
    cdb = jnp.exp(lgb * C)

    tdims = (((0,), (0,)), ((), ()))
    ndims = (((1,), (1,)), ((), ()))

    def chunk_rows(c):
        return pl.ds(pl.multiple_of(c * C, C), C)

    def scan_body(t, carry):
        sf, sb = carry
        cf = t
        cb = nc - 1 - t
        st_scr[cf, 0:C, :] = sf.astype(BF16)
        st_scr[cb, C:2 * C, :] = sb.astype(BF16)
        rf = chunk_rows(cf)
        rb = chunk_rows(cb)
        kzf = (k_ref[rf, :].astype(F32) * tab_scr[3]).astype(BF16)
        kzb = (k_ref[rb, :].astype(F32) * tab_scr[4]).astype(BF16)
        sf = sf * cdf + lax.dot_general(kzf, v_ref[rf, :], tdims, preferred_element_type=F32)
        sb = sb * cdb + lax.dot_general(kzb, v_ref[rb, :], tdims, preferred_element_type=F32)
        return sf, sb

    zero = jnp.zeros((C, C), F32)
    lax.fori_loop(0, nc, scan_body, (zero, zero), unroll=RET_UNROLL)

    def out_body(t, carry):
        cs = [t * RET_UNROLL + i for i in range(RET_UNROLL)]
        rows = [chunk_rows(c) for c in cs]
        qs = [q_ref[r, :] for r in rows]
        ss = [lax.dot_general(q, k_ref[r, :], ndims, preferred_element_type=F32) for q, r in zip(qs, rows)]
        atts = [(s * tab_scr[0]).astype(BF16) for s in ss]
        qxs = []
        for q in qs:
            qf = q.astype(F32)
            qxs.append(jnp.concatenate([(qf * tab_scr[1]).astype(BF16), (qf * tab_scr[2]).astype(BF16)], axis=1))
        ys = [jnp.dot(att, v_ref[r, :], preferred_element_type=F32)
              + jnp.dot(qx, st_scr[c], preferred_element_type=F32)
              for att, qx, r, c in zip(atts, qxs, rows, cs)]
        for y, r in zip(ys, rows):
            var = jnp.mean(y * y, axis=-1, keepdims=True)
            g = g_ref[r, :].astype(F32)
            gate = g * (1.0 / (1.0 + jnp.exp(-g)))
            o_ref[r, :] = (y * lax.rsqrt(var + RMS_EPS) * gate).astype(o_ref.dtype)
        return carry

    lax.fori_loop(0, nc // RET_UNROLL, out_body, 0)


def _retention(proj, dec_f, dec_b, ret_heads, seq):
    n = proj.shape[0]
    nc = seq // RET_CHUNK
    dec_f = jnp.broadcast_to(dec_f.astype(F32).reshape(ret_heads, 1, 1), (ret_heads, 8, HEAD_DIM))
    dec_b = jnp.broadcast_to(dec_b.astype(F32).reshape(ret_heads, 1, 1), (ret_heads, 8, HEAD_DIM))
    dec_spec = pl.BlockSpec((1, 8, HEAD_DIM), lambda b, h: (h, 0, 0))

    def col_spec(base):
        return pl.BlockSpec((seq, HEAD_DIM), lambda b, h: (b, base + h))

    return pl.pallas_call(
        functools.partial(_ret_kernel, nc=nc),
        grid=(n // seq, ret_heads),
        in_specs=[dec_spec, dec_spec, col_spec(0), col_spec(ret_heads), col_spec(2 * ret_heads),
                  col_spec(3 * ret_heads)],
        out_specs=pl.BlockSpec((seq, HEAD_DIM), lambda b, h: (b, h)),
        out_shape=jax.ShapeDtypeStruct((n, ret_heads * HEAD_DIM), BF16),
        scratch_shapes=[pltpu.VMEM((nc, 2 * RET_CHUNK, RET_CHUNK), BF16),
                        pltpu.VMEM((5, RET_CHUNK, RET_CHUNK), F32)],
        compiler_params=_params(("parallel", "arbitrary")),
        name="retention",
    )(dec_f, dec_b, proj, proj, proj, proj)


SWA_MAX_QBLOCKS = 8


def _swa_kernel(sink_ref, q_ref, k_ref, v_ref, bias_ref, o_ref, *, nblk):
    B = BLOCK
    SWA_QBLOCKS = q_ref.shape[0] // B
    kvh = pl.program_id(1)
    step = pl.program_id(2)
    ndims = (((1,), (1,)), ((), ()))
    scale = HEAD_DIM ** -0.5
    exp2_coef = scale * math.log2(math.e)
    bands = []
    scores = []
    for qb in range(SWA_QBLOCKS):
        n = step * SWA_QBLOCKS + qb
        variant = jnp.where(n > 0, 0, 1) + jnp.where(n < nblk - 1, 0, 2)
        rp = pl.ds(pl.multiple_of(jnp.maximum(n - 1, 0) * B, B), B)
        rc = pl.ds(pl.multiple_of(n * B, B), B)
        rn = pl.ds(pl.multiple_of(jnp.minimum(n + 1, nblk - 1) * B, B), B)
        kband = jnp.concatenate([k_ref[rp, :], k_ref[rc, :], k_ref[rn, :]], axis=0)
        vband = jnp.concatenate([v_ref[rp, :], v_ref[rc, :], v_ref[rn, :]], axis=0)
        bands.append((vband, variant))
        for g in range(GQA_GROUP):
            q = q_ref[qb * B:(qb + 1) * B, g * HEAD_DIM:(g + 1) * HEAD_DIM]
            scores.append(lax.dot_general(q, kband, ndims, preferred_element_type=F32))
    probs = []
    for idx, s in enumerate(scores):
        qb, g = divmod(idx, GQA_GROUP)
        sink = sink_ref[kvh * GQA_GROUP + g] * (1.0 / scale)
        t = s + bias_ref[bands[qb][1], g]
        mx = jnp.maximum(jnp.max(t, axis=-1, keepdims=True), sink)
        p = jnp.exp2((t - mx) * exp2_coef)
        denom = jnp.sum(p, axis=-1, keepdims=True) + jnp.exp2((sink - mx) * exp2_coef)
        probs.append((p.astype(BF16), 1.0 / denom))
    for idx, (p, inv) in enumerate(probs):
        qb, g = divmod(idx, GQA_GROUP)
        o = jnp.dot(p, bands[qb][0], preferred_element_type=F32) * inv
        o_ref[qb * B:(qb + 1) * B, g * HEAD_DIM:(g + 1) * HEAD_DIM] = o.astype(o_ref.dtype)


def _swa(proj, sink, bias, ret_w, att_heads, seq):
    n = proj.shape[0]
    kv_heads = att_heads // GQA_GROUP
    nblk = seq // BLOCK
    qblocks = math.gcd(nblk, SWA_MAX_QBLOCKS)
    steps = nblk // qblocks
    qrows = qblocks * BLOCK
    gw = GQA_GROUP * HEAD_DIM
    q_base = 4 * ret_w // gw
    k_base = (4 * ret_w + att_heads * HEAD_DIM) // HEAD_DIM
    v_base = k_base + kv_heads
    return pl.pallas_call(
        functools.partial(_swa_kernel, nblk=nblk),
        grid=(n // seq, kv_heads, steps),
        in_specs=[pl.BlockSpec(memory_space=pltpu.SMEM),
                  pl.BlockSpec((qrows, gw), lambda b, kv, i: (b * steps + i, q_base + kv)),
                  pl.BlockSpec((seq, HEAD_DIM), lambda b, kv, i: (b, k_base + kv)),
                  pl.BlockSpec((seq, HEAD_DIM), lambda b, kv, i: (b, v_base + kv)),
                  pl.BlockSpec((4, GQA_GROUP, BLOCK, 3 * BLOCK), lambda b, kv, i: (0, kv, 0, 0))],
        out_specs=pl.BlockSpec((qrows, gw), lambda b, kv, i: (b * steps + i, kv)),
        out_shape=jax.ShapeDtypeStruct((n, att_heads * HEAD_DIM), BF16),
        compiler_params=_params(("parallel", "arbitrary", "arbitrary")),
        name="swa",
    )(sink.astype(F32), proj, proj, proj, bias)


def _t5_buckets(rel):
    nb = N_BUCKETS // 2
    max_exact = nb // 2
    base = np.where(rel > 0, nb, 0)
    n = np.abs(rel)
    large = max_exact + (np.log(np.maximum(n, 1) / max_exact) / np.log(MAX_DISTANCE / max_exact)
                         * (nb - max_exact)).astype(np.int32)
    large = np.minimum(large, nb - 1)
    return (base + np.where(n < max_exact, n, large)).astype(np.int32)


def _band_bias(rel_bias):
    qi = np.arange(BLOCK)[:, None]
    kj = np.arange(3 * BLOCK)[None, :] - BLOCK
    rel = kj - qi
    buckets = jnp.asarray(_t5_buckets(rel).reshape(-1, 1))
    onehot = (buckets == jnp.arange(N_BUCKETS, dtype=jnp.int32)[None, :]).astype(F32)
    table = jnp.dot(onehot, rel_bias.astype(F32), precision=lax.Precision.HIGHEST)
    table = jnp.transpose(table).reshape(rel_bias.shape[1], BLOCK, 3 * BLOCK) * (HEAD_DIM ** 0.5)
    in_window = np.abs(rel) <= WINDOW
    variants = []
    for v in range(4):
        ok = in_window & ((kj >= 0) | (v & 1 == 0)) & ((kj < BLOCK) | (v & 2 == 0))
        variants.append(jnp.where(jnp.asarray(ok)[None], table, NEG_INF))
    return jnp.stack(variants)


CROSS_SUBTILES = 2


def _cross_kernel(x_ref, gc_ref, gm_ref, wq_ref, kv_ref, wo_ref, x2_ref, h3_ref):
    ndims = (((1,), (1,)), ((), ()))
    mem_w = MEM_HEADS * HEAD_DIM
    tm = x_ref.shape[0]
    nsub = CROSS_SUBTILES if tm % (16 * CROSS_SUBTILES) == 0 else 1
    subs = [slice(t * (tm // nsub), (t + 1) * (tm // nsub)) for t in range(nsub)]
    head_cols = [slice(h * HEAD_DIM, (h + 1) * HEAD_DIM) for h in range(MEM_HEADS)]

    hns = []
    for rows in subs:
        x = x_ref[rows, :]
        var = jnp.mean(x * x, axis=-1, keepdims=True)
        hns.append((x * lax.rsqrt(var + RMS_EPS) * gc_ref[...]).astype(BF16))
    qs = [jnp.dot(hn, wq_ref[...], preferred_element_type=F32).astype(BF16) for hn in hns]
    scores = [[lax.dot_general(q[:, c], kv_ref[:, c], ndims, preferred_element_type=F32) for c in head_cols]
              for q in qs]
    probs = []
    for sub_scores in scores:
        sub_probs = []
        for s in sub_scores:
            s = s * (HEAD_DIM ** -0.5)
            p = jnp.exp(s - jnp.max(s, axis=-1, keepdims=True))
            sub_probs.append((p.astype(BF16), 1.0 / jnp.sum(p, axis=-1, keepdims=True)))
        probs.append(sub_probs)
    os = []
    for sub_probs in probs:
        outs = []
        for h, (p, inv) in enumerate(sub_probs):
            v = kv_ref[:, mem_w + h * HEAD_DIM:mem_w + (h + 1) * HEAD_DIM]
            outs.append((jnp.dot(p, v, preferred_element_type=F32) * inv).astype(BF16))
        os.append(jnp.concatenate(outs, axis=1))
    x2s = [jnp.dot(o, wo_ref[...], preferred_element_type=F32) + x_ref[rows, :] for o, rows in zip(os, subs)]
    for x2, rows in zip(x2s, subs):
        x2_ref[rows, :] = x2
        var2 = jnp.mean(x2 * x2, axis=-1, keepdims=True)
        h3_ref[rows, :] = (x2 * lax.rsqrt(var2 + RMS_EPS) * gm_ref[...]).astype(h3_ref.dtype)


def _cross_block(x, g_cross, g_mlp, w_cq, kvm, w_co, mem_tokens, seq):
    n, d = x.shape
    mem_w = w_cq.shape[1]
    tm = _pick(seq, (512, 256, 128, 64, 32, 16, 8))
    row_spec = pl.BlockSpec((tm, d), lambda i: (i, 0))
    gain_spec = pl.BlockSpec((1, d), lambda i: (0, 0))
    resident = pl.Buffered(1)
    return pl.pallas_call(
        _cross_kernel,
        grid=(n // tm,),
        in_specs=[row_spec, gain_spec, gain_spec,
                  pl.BlockSpec((d, mem_w), lambda i: (0, 0), pipeline_mode=resident),
                  pl.BlockSpec((mem_tokens, 2 * mem_w), lambda i: (i // (seq // tm), 0)),
                  pl.BlockSpec((mem_w, d), lambda i: (0, 0), pipeline_mode=resident)],
        out_specs=[row_spec, row_spec],
        out_shape=[jax.ShapeDtypeStruct((n, d), F32), jax.ShapeDtypeStruct((n, d), BF16)],
        compiler_params=_params(("parallel",)),
        name="cross_block",
    )(x, g_cross.reshape(1, d).astype(F32), g_mlp.reshape(1, d).astype(F32), w_cq, kvm, w_co)


def _outproj_kernel(a1_ref, a2_ref, b1_ref, b2_ref, r_ref, o_ref):
    acc = jnp.dot(a1_ref[...], b1_ref[...], preferred_element_type=F32)
    acc = acc + jnp.dot(a2_ref[...], b2_ref[...], preferred_element_type=F32)
    o_ref[...] = acc + r_ref[...]


def _outproj(y_ret, y_att, w_out, x):
    m, k1 = y_ret.shape
    _, k2 = y_att.shape
    assert k1 % k2 == 0
    n = w_out.shape[1]
    tm = _pick(m, ROW_TILES)
    tn = _pick(n, COL_TILES)
    return pl.pallas_call(
        _outproj_kernel,
        grid=(m // tm, n // tn),
        in_specs=[pl.BlockSpec((tm, k1), lambda i, j: (i, 0)),
                  pl.BlockSpec((tm, k2), lambda i, j: (i, 0)),
                  pl.BlockSpec((k1, tn), lambda i, j: (0, j)),
                  pl.BlockSpec((k2, tn), lambda i, j: (k1 // k2, j)),
                  pl.BlockSpec((tm, tn), lambda i, j: (i, j))],
        out_specs=pl.BlockSpec((tm, tn), lambda i, j: (i, j)),
        out_shape=jax.ShapeDtypeStruct((m, n), F32),
        compiler_params=_params(("parallel", "arbitrary")),
        name="outproj",
    )(y_ret, y_att, w_out, w_out, x)


def _mm_kacc_kernel(*refs):
    (a_ref, b_ref, r_ref), (o_ref,), side = _split_side(refs, 3, 1)
    kk = pl.program_id(2)

    def _dot():
        if side is not None:
            _side_body(*side)
        return jnp.dot(a_ref[...], b_ref[...], preferred_element_type=F32)

    @pl.when(kk == 0)
    def _():
        o_ref[...] = r_ref[...] + _dot()

    @pl.when(kk != 0)
    def _():
        o_ref[...] += _dot()


def _matmul_kacc(a, b, residual, side=None):
    m, k = a.shape
    _, n = b.shape
    tm = _pick(m, ROW_TILES)
    tn = _pick(n, COL_TILES)
    tk = _pick(k, (4096, 2048) + COL_TILES)
    nj, nk = n // tn, k // tk
    return _call_with_side(
        _mm_kacc_kernel, "matmul_kacc", (m // tm, nj, nk), ("parallel", "arbitrary", "arbitrary"),
        [pl.BlockSpec((tm, tk), lambda i, j, kk: (i, kk)), pl.BlockSpec((tk, tn), lambda i, j, kk: (kk, j)),
         pl.BlockSpec((tm, tn), lambda i, j, kk: (i, j))],
        [a, b, residual], pl.BlockSpec((tm, tn), lambda i, j, kk: (i, j)),
        jax.ShapeDtypeStruct((m, n), F32),
        side, (m // tm) * nj * nk, lambda i, j, kk: (i * nj + j) * nk + kk)


def _rope_tables(seq):
    half = HEAD_DIM // 2
    inv = ROPE_BASE ** (-jnp.arange(half, dtype=F32) / half)
    ang = jnp.arange(seq, dtype=F32)[:, None] * inv[None, :]
    cos = jnp.cos(ang)
    sin = jnp.sin(ang)
    return jnp.concatenate([cos, cos], axis=-1), jnp.concatenate([-sin, sin], axis=-1)


def _layer(x, h, mem, w, l, bias, rope, seq, mem_tokens, mlp_weights, side_mlp_in=None, side_mlp_out=None):
    ret_heads = w["dec_f"][l].shape[0]
    att_heads = w["sink"][l].shape[0]
    ret_w = ret_heads * HEAD_DIM
    if mlp_weights is None:
        assert side_mlp_in is None
        proj, w1 = _inproj(h, w["w_in"][l], *rope, ret_w, seq, side=_SideJob(w["w_mlp_in"][l]))
        side_mlp_in = _SideJob(w["w_mlp_out"][l])
    else:
        proj, _ = _inproj(h, w["w_in"][l], *rope, ret_w, seq)
        w1 = mlp_weights[0]
    y_ret = _retention(proj, w["dec_f"][l], w["dec_b"][l], ret_heads, seq)
    y_att = _swa(proj, w["sink"][l], bias, ret_w, att_heads, seq)
    x = _outproj(y_ret, y_att, w["w_out"][l], x)

    hm = _rmsnorm(mem, w["norm_mem"][l], BF16)
    kvm, _ = _matmul(hm, w["w_ckv"][l], BF16)
    x, h = _cross_block(x, w["norm_cross"][l], w["norm_mlp"][l], w["w_cq"][l], kvm, w["w_co"][l],
                        mem_tokens, seq)

    a, side1 = _matmul(h, w1, BF16, epilogue="relu2", side=side_mlp_in)
    if mlp_weights is None:
        mlp_weights, side1 = (w1, side1), None
    x, side2 = _matmul_kacc(a, mlp_weights[1], x, side=side_mlp_out)
    return x, mlp_weights, side1, side2


def kernel(x_prompt, x_sample, mem_prompt, mem_sample, norm_mix, w_in, ret_decay_f, ret_decay_b, attn_sink,
           rel_bias, w_out, norm_cross, norm_mem, w_cq, w_ckv, w_co, norm_mlp, w_mlp_in, w_mlp_out, norm_final):
    depth = w_in.shape[0]
    w = dict(norm_mix=norm_mix, w_in=w_in.astype(BF16), dec_f=ret_decay_f, dec_b=ret_decay_b, sink=attn_sink,
             w_out=w_out.astype(BF16), norm_cross=norm_cross, norm_mem=norm_mem, w_cq=w_cq.astype(BF16),
             w_ckv=w_ckv.astype(BF16), w_co=w_co.astype(BF16), norm_mlp=norm_mlp, w_mlp_in=w_mlp_in,
             w_mlp_out=w_mlp_out)
    bias = _band_bias(rel_bias)
    d = x_prompt.shape[-1]
    mem_tokens = mem_prompt.shape[1]
    seq_a, seq_b = x_prompt.shape[1], x_sample.shape[1]
    xa, xb = x_prompt.reshape(-1, d), x_sample.reshape(-1, d)
    ma, mb = mem_prompt.reshape(-1, d), mem_sample.reshape(-1, d)
    rope_a, rope_b = _rope_tables(seq_a), _rope_tables(seq_b)

    mlp_w = []
    hb = None
    for l in range(depth):
        ha = _rmsnorm(xa, norm_mix[l], BF16)
        side = _SideJob(xb, norm_mix[0], BF16) if l == 0 else None
        xa, wl, _, side_out = _layer(xa, ha, ma, w, l, bias, rope_a, seq_a, mem_tokens, None, side_mlp_out=side)
        mlp_w.append(wl)
        if l == 0:
            hb = side_out
    ya = None
    for l in range(depth):
        if l > 0:
            hb = _rmsnorm(xb, norm_mix[l], BF16)
        side = _SideJob(xa, norm_final, F32) if l == 0 else None
        xb, _, side_out, _ = _layer(xb, hb, mb, w, l, bias, rope_b, seq_b, mem_tokens, mlp_w[l], side_mlp_in=side)
        if l == 0:
            ya = side_out
    yb = _rmsnorm(xb, norm_final, F32)
    return ya.reshape(x_prompt.shape), yb.reshape(x_sample.shape)
```

```python
import functools
import math

import numpy as np
import jax
import jax.numpy as jnp
from jax import lax
from jax.experimental import pallas as pl
from jax.experimental.pallas import tpu as pltpu

HEAD_DIM = 128
RET_CHUNK = 128
WINDOW = 128
BLOCK = 128
N_BUCKETS = 32
MAX_DISTANCE = 128
MEM_HEADS = 4
GQA_GROUP = 4
ROPE_BASE = 10000.0
RMS_EPS = 1e-6
NEG_INF = -1e30

VMEM_LIMIT_BYTES = 60 * 1024 * 1024

BF16 = jnp.bfloat16
F32 = jnp.float32

ROW_TILES = (1024, 512, 256, 128, 64, 32, 16, 8)
COL_TILES = (1024, 512, 256, 128)


def _pick(n, prefs):
    for p in prefs:
        if n % p == 0:
            return p
    raise ValueError(f"no tile in {prefs} divides {n}")


def _params(sem):
    return pltpu.CompilerParams(dimension_semantics=sem, vmem_limit_bytes=VMEM_LIMIT_BYTES)


def _rmsnorm_kernel(x_ref, g_ref, o_ref):
    x = x_ref[...]
    var = jnp.mean(x * x, axis=-1, keepdims=True)
    o_ref[...] = (x * lax.rsqrt(var + RMS_EPS) * g_ref[...]).astype(o_ref.dtype)


def _rmsnorm(x, g, out_dtype):
    n, d = x.shape
    tm = _pick(n, (256, 128, 64, 32, 16, 8))
    return pl.pallas_call(
        _rmsnorm_kernel,
        grid=(n // tm,),
        in_specs=[pl.BlockSpec((tm, d), lambda i: (i, 0)),
                  pl.BlockSpec((1, d), lambda i: (0, 0))],
        out_specs=pl.BlockSpec((tm, d), lambda i: (i, 0)),
        out_shape=jax.ShapeDtypeStruct((n, d), out_dtype),
        compiler_params=_params(("parallel",)),
        name="rmsnorm",
    )(x, g.reshape(1, d).astype(F32))


class _SideJob:
    def __init__(self, src, gain=None, out_dtype=BF16):
        self.src = src
        self.gain = gain
        self.out_dtype = out_dtype

    def plan(self, nblocks, block_of_step):
        rows, cols = self.src.shape
        if rows % nblocks or (rows // nblocks) % 16:
            return None
        spec = pl.BlockSpec((rows // nblocks, cols), lambda *ids: (block_of_step(*ids), 0))
        in_specs, args = [spec], [self.src]
        if self.gain is not None:
            in_specs.append(pl.BlockSpec((1, cols), lambda *ids: (0, 0)))
            args.append(self.gain.reshape(1, cols).astype(F32))
        return in_specs, args, spec, jax.ShapeDtypeStruct(self.src.shape, self.out_dtype)

    def standalone(self):
        if self.gain is None:
            return self.src.astype(self.out_dtype)
        return _rmsnorm(self.src, self.gain, self.out_dtype)


def _side_body(in_refs, out_ref):
    if len(in_refs) == 1:
        out_ref[...] = in_refs[0][...].astype(out_ref.dtype)
    else:
        _rmsnorm_kernel(in_refs[0], in_refs[1], out_ref)


def _split_side(refs, n_main_in, n_main_out, side_arity):
    main_in = refs[:n_main_in]
    pos = n_main_in
    side_ins = []
    for n in side_arity:
        side_ins.append(refs[pos:pos + n])
        pos += n
    main_out = refs[pos:pos + n_main_out]
    side_outs = refs[pos + n_main_out:]
    return main_in, main_out, list(zip(side_ins, side_outs))


def _run_sides(sides):
    for in_refs, out_ref in sides:
        _side_body(in_refs, out_ref)


def _call_with_side(kernel_fn, name, grid, sem, in_specs, args, out_spec, out_shape, sides, nblocks,
                    block_of_step):
    in_specs, args = list(in_specs), list(args)
    out_specs, out_shapes, arity, hosted = [out_spec], [out_shape], [], []
    for job in sides:
        plan = job.plan(nblocks, block_of_step)
        hosted.append(plan is not None)
        if plan is not None:
            s_in_specs, s_args, s_out_spec, s_out_shape = plan
            in_specs += s_in_specs
            args += s_args
            out_specs.append(s_out_spec)
            out_shapes.append(s_out_shape)
            arity.append(len(s_in_specs))
    outs = pl.pallas_call(
        functools.partial(kernel_fn, side_arity=tuple(arity)), grid=grid, in_specs=in_specs,
        out_specs=out_specs, out_shape=out_shapes, compiler_params=_params(sem),
        name=name + ("_side" if arity else ""),
    )(*args)
    hosted_outs = iter(outs[1:])
    return outs[0], [next(hosted_outs) if h else job.standalone() for job, h in zip(sides, hosted)]


def _mm_kernel(*refs, epilogue, side_arity):
    (a_ref, b_ref), (o_ref,), sides = _split_side(refs, 2, 1, side_arity)
    acc = jnp.dot(a_ref[...], b_ref[...], preferred_element_type=F32)
    if epilogue == "relu2":
        acc = jnp.maximum(acc, 0.0)
        acc = acc * acc
    o_ref[...] = acc.astype(o_ref.dtype)
    _run_sides(sides)


def _matmul(a, b, out_dtype, epilogue="none", sides=()):
    m, k = a.shape
    _, n = b.shape
    tm = _pick(m, ROW_TILES)
    tn = _pick(n, COL_TILES)
    nj = n // tn
    return _call_with_side(
        functools.partial(_mm_kernel, epilogue=epilogue), "matmul_" + epilogue,
        (m // tm, nj), ("parallel", "arbitrary"),
        [pl.BlockSpec((tm, k), lambda i, j: (i, 0)), pl.BlockSpec((k, tn), lambda i, j: (0, j))],
        [a, b], pl.BlockSpec((tm, tn), lambda i, j: (i, j)), jax.ShapeDtypeStruct((m, n), out_dtype),
        sides, (m // tm) * nj, lambda i, j: i * nj + j)


def _inproj_kernel(*refs, q_tiles, k_tiles, side_arity):
    (a_ref, b_ref, cos_ref, sin_ref), (o_ref,), sides = _split_side(refs, 4, 1, side_arity)
    j = pl.program_id(1)

    def _dot():
        _run_sides(sides)
        return jnp.dot(a_ref[...], b_ref[...], preferred_element_type=F32)

    def _store_rope(scale):
        acc = _dot()
        cos = cos_ref[...]
        sin = sin_ref[...]
        if scale != 1.0:
            cos = cos * scale
            sin = sin * scale
        for h in range(acc.shape[1] // HEAD_DIM):
            cols = slice(h * HEAD_DIM, (h + 1) * HEAD_DIM)
            x = acc[:, cols]
            o_ref[:, cols] = (x * cos + pltpu.roll(x, HEAD_DIM // 2, 1) * sin).astype(o_ref.dtype)

    @pl.when(j < q_tiles)
    def _():
        _store_rope(1.0)

    @pl.when(jnp.logical_and(j >= q_tiles, j < q_tiles + k_tiles))
    def _():
        _store_rope(HEAD_DIM ** -0.5)

    @pl.when(j >= q_tiles + k_tiles)
    def _():
        o_ref[...] = _dot().astype(o_ref.dtype)


def _inproj(h, w_in, cos_t, sin_t, ret_w, seq, sides=()):
    m, k = h.shape
    _, n = w_in.shape
    tn = _pick(math.gcd(n, ret_w), COL_TILES)
    tm = _pick(seq, ROW_TILES)
    nj = n // tn
    nj_side = 1 << (nj.bit_length() - 1)
    pos_spec = pl.BlockSpec((tm, HEAD_DIM), lambda i, j: (i % (seq // tm), 0))
    return _call_with_side(
        functools.partial(_inproj_kernel, q_tiles=ret_w // tn, k_tiles=ret_w // tn), "inproj_rope",
        (m // tm, nj), ("parallel", "arbitrary"),
        [pl.BlockSpec((tm, k), lambda i, j: (i, 0)), pl.BlockSpec((k, tn), lambda i, j: (0, j)),
         pos_spec, pos_spec],
        [h, w_in, cos_t, sin_t], pl.BlockSpec((tm, tn), lambda i, j: (i, j)),
        jax.ShapeDtypeStruct((m, n), BF16),
        sides, (m // tm) * nj_side, lambda i, j: i * nj_side + jnp.minimum(j, nj_side - 1))


RET_MAX_UNROLL = 16


def _ret_kernel(decf_ref, decb_ref, q_ref, k_ref, v_ref, g_ref, o_ref, st_scr, tab_scr, *, nc):
    C = RET_CHUNK
    RET_UNROLL = math.gcd(nc, RET_MAX_UNROLL)
    lgf = -jnp.exp(jnp.broadcast_to(decf_ref[0, 0:1, :], (C, C)))
    lgb = -jnp.exp(jnp.broadcast_to(decb_ref[0, 0:1, :], (C, C)))
    row = lax.broadcasted_iota(jnp.int32, (C, C), 0).astype(F32)
    col = lax.broadcasted_iota(jnp.int32, (C, C), 1).astype(F32)
    diff = row - col
    tab_scr[0] = jnp.where(diff >= 0, jnp.exp(lgf * jnp.maximum(diff, 0.0)),
                           jnp.exp(lgb * jnp.maximum(-diff, 0.0)))
    tab_scr[1] = jnp.exp(lgf * (row + 1.0))
    tab_scr[2] = jnp.exp(lgb * (C - row))
    tab_scr[3] = jnp.exp(lgf * (C - 1.0 - row))
    tab_scr[4] = jnp.exp(lgb * row)
    cdf = jnp.exp(lgf * C)
    cdb = jnp.exp(lgb * C)

    tdims = (((0,), (0,)), ((), ()))
    ndims = (((1,), (1,)), ((), ()))

    def chunk_rows(c):
        return pl.ds(pl.multiple_of(c * C, C), C)

    def scan_body(t, carry):
        sf, sb = carry
        cf = t
        cb = nc - 1 - t
        st_scr[cf, 0:C, :] = sf.astype(BF16)
        st_scr[cb, C:2 * C, :] = sb.astype(BF16)
        rf = chunk_rows(cf)
        rb = chunk_rows(cb)
        kzf = (k_ref[rf, :].astype(F32) * tab_scr[3]).astype(BF16)
        kzb = (k_ref[rb, :].astype(F32) * tab_scr[4]).astype(BF16)
        sf = sf * cdf + lax.dot_general(kzf, v_ref[rf, :], tdims, preferred_element_type=F32)
        sb = sb * cdb + lax.dot_general(kzb, v_ref[rb, :], tdims, preferred_element_type=F32)
        return sf, sb

    zero = jnp.zeros((C, C), F32)
    lax.fori_loop(0, nc, scan_body, (zero, zero), unroll=RET_UNROLL)

    def out_body(t, carry):
        cs = [t * RET_UNROLL + i for i in range(RET_UNROLL)]
        rows = [chunk_rows(c) for c in cs]
        qs = [q_ref[r, :] for r in rows]
        ss = [lax.dot_general(q, k_ref[r, :], ndims, preferred_element_type=F32) for q, r in zip(qs, rows)]
        atts = [(s * tab_scr[0]).astype(BF16) for s in ss]
        lhss = []
        for q, att in zip(qs, atts):
            qf = q.astype(F32)
            lhss.append(jnp.concatenate([att, (qf * tab_scr[1]).astype(BF16), (qf * tab_scr[2]).astype(BF16)],
                                        axis=1))
        ys = [jnp.dot(lhs, jnp.concatenate([v_ref[r, :], st_scr[c]], axis=0), preferred_element_type=F32)
              for lhs, r, c in zip(lhss, rows, cs)]
        for y, r in zip(ys, rows):
            var = jnp.mean(y * y, axis=-1, keepdims=True)
            half_g = 0.5 * g_ref[r, :].astype(F32)
            gate = half_g + half_g * jnp.tanh(half_g)
            o_ref[r, :] = (y * lax.rsqrt(var + RMS_EPS) * gate).astype(o_ref.dtype)
        return carry

    lax.fori_loop(0, nc // RET_UNROLL, out_body, 0)


def _retention(proj, dec_f, dec_b, ret_heads, seq):
    n = proj.shape[0]
    nc = seq // RET_CHUNK
    dec_f = jnp.broadcast_to(dec_f.astype(F32).reshape(ret_heads, 1, 1), (ret_heads, 8, HEAD_DIM))
    dec_b = jnp.broadcast_to(dec_b.astype(F32).reshape(ret_heads, 1, 1), (ret_heads, 8, HEAD_DIM))
    dec_spec = pl.BlockSpec((1, 8, HEAD_DIM), lambda b, h: (h, 0, 0))

    def col_spec(base):
        return pl.BlockSpec((seq, HEAD_DIM), lambda b, h: (b, base + h))

    return pl.pallas_call(
        functools.partial(_ret_kernel, nc=nc),
        grid=(n // seq, ret_heads),
        in_specs=[dec_spec, dec_spec, col_spec(0), col_spec(ret_heads), col_spec(2 * ret_heads),
                  col_spec(3 * ret_heads)],
        out_specs=pl.BlockSpec((seq, HEAD_DIM), lambda b, h: (b, h)),
        out_shape=jax.ShapeDtypeStruct((n, ret_heads * HEAD_DIM), BF16),
        scratch_shapes=[pltpu.VMEM((nc, 2 * RET_CHUNK, RET_CHUNK), BF16),
                        pltpu.VMEM((5, RET_CHUNK, RET_CHUNK), F32)],
        compiler_params=_params(("parallel", "arbitrary")),
        name="retention",
    )(dec_f, dec_b, proj, proj, proj, proj)


SWA_MAX_QBLOCKS = 8


def _swa_kernel(sink_ref, q_ref, k_ref, v_ref, bias_ref, o_ref, *, nblk):
    B = BLOCK
    SWA_QBLOCKS = q_ref.shape[0] // B
    kvh = pl.program_id(1)
    step = pl.program_id(2)
    ndims = (((1,), (1,)), ((), ()))
    scale = HEAD_DIM ** -0.5
    exp2_coef = scale * math.log2(math.e)
    bands = []
    scores = []
    for qb in range(SWA_QBLOCKS):
        n = step * SWA_QBLOCKS + qb
        variant = jnp.where(n > 0, 0, 1) + jnp.where(n < nblk - 1, 0, 2)
        rp = pl.ds(pl.multiple_of(jnp.maximum(n - 1, 0) * B, B), B)
        rc = pl.ds(pl.multiple_of(n * B, B), B)
        rn = pl.ds(pl.multiple_of(jnp.minimum(n + 1, nblk - 1) * B, B), B)
        kband = jnp.concatenate([k_ref[rp, :], k_ref[rc, :], k_ref[rn, :]], axis=0)
        vband = jnp.concatenate([v_ref[rp, :], v_ref[rc, :], v_ref[rn, :]], axis=0)
        bands.append((vband, variant))
        for g in range(GQA_GROUP):
            q = q_ref[qb * B:(qb + 1) * B, g * HEAD_DIM:(g + 1) * HEAD_DIM]
            scores.append(lax.dot_general(q, kband, ndims, preferred_element_type=F32))
    probs = []
    for idx, s in enumerate(scores):
        qb, g = divmod(idx, GQA_GROUP)
        sink = sink_ref[kvh * GQA_GROUP + g] * (1.0 / scale)
        t = s + bias_ref[bands[qb][1], g]
        mx = jnp.maximum(jnp.max(t, axis=-1, keepdims=True), sink)
        p = jnp.exp2((t - mx) * exp2_coef)
        denom = jnp.sum(p, axis=-1, keepdims=True) + jnp.exp2((sink - mx) * exp2_coef)
        probs.append((p.astype(BF16), 1.0 / denom))
    for idx, (p, inv) in enumerate(probs):
        qb, g = divmod(idx, GQA_GROUP)
        o = jnp.dot(p, bands[qb][0], preferred_element_type=F32) * inv
        o_ref[qb * B:(qb + 1) * B, g * HEAD_DIM:(g + 1) * HEAD_DIM] = o.astype(o_ref.dtype)


def _swa(proj, sink, bias, ret_w, att_heads, seq):
    n = proj.shape[0]
    kv_heads = att_heads // GQA_GROUP
    nblk = seq // BLOCK
    qblocks = math.gcd(nblk, SWA_MAX_QBLOCKS)
    steps = nblk // qblocks
    qrows = qblocks * BLOCK
    gw = GQA_GROUP * HEAD_DIM
    q_base = 4 * ret_w // gw
    k_base = (4 * ret_w + att_heads * HEAD_DIM) // HEAD_DIM
    v_base = k_base + kv_heads
    return pl.pallas_call(
        functools.partial(_swa_kernel, nblk=nblk),
        grid=(n // seq, kv_heads, steps),
        in_specs=[pl.BlockSpec(memory_space=pltpu.SMEM),
                  pl.BlockSpec((qrows, gw), lambda b, kv, i: (b * steps + i, q_base + kv)),
                  pl.BlockSpec((seq, HEAD_DIM), lambda b, kv, i: (b, k_base + kv)),
                  pl.BlockSpec((seq, HEAD_DIM), lambda b, kv, i: (b, v_base + kv)),
                  pl.BlockSpec((4, GQA_GROUP, BLOCK, 3 * BLOCK), lambda b, kv, i: (0, kv, 0, 0))],
        out_specs=pl.BlockSpec((qrows, gw), lambda b, kv, i: (b * steps + i, kv)),
        out_shape=jax.ShapeDtypeStruct((n, att_heads * HEAD_DIM), BF16),
        compiler_params=_params(("parallel", "arbitrary", "arbitrary")),
        name="swa",
    )(sink.astype(F32), proj, proj, proj, bias)


def _t5_buckets(rel):
    nb = N_BUCKETS // 2
    max_exact = nb // 2
    base = np.where(rel > 0, nb, 0)
    n = np.abs(rel)
    large = max_exact + (np.log(np.maximum(n, 1) / max_exact) / np.log(MAX_DISTANCE / max_exact)
                         * (nb - max_exact)).astype(np.int32)
    large = np.minimum(large, nb - 1)
    return (base + np.where(n < max_exact, n, large)).astype(np.int32)


def _band_bias(rel_bias):
    qi = np.arange(BLOCK)[:, None]
    kj = np.arange(3 * BLOCK)[None, :] - BLOCK
    rel = kj - qi
    buckets = jnp.asarray(_t5_buckets(rel).reshape(-1, 1))
    onehot = (buckets == jnp.arange(N_BUCKETS, dtype=jnp.int32)[None, :]).astype(F32)
    table = jnp.dot(onehot, rel_bias.astype(F32), precision=lax.Precision.HIGHEST)
    table = jnp.transpose(table).reshape(rel_bias.shape[1], BLOCK, 3 * BLOCK) * (HEAD_DIM ** 0.5)
    in_window = np.abs(rel) <= WINDOW
    variants = []
    for v in range(4):
        ok = in_window & ((kj >= 0) | (v & 1 == 0)) & ((kj < BLOCK) | (v & 2 == 0))
        variants.append(jnp.where(jnp.asarray(ok)[None], table, NEG_INF))
    return jnp.stack(variants)


CROSS_SUBTILES = 2


def _cross_kernel(x_ref, gc_ref, gm_ref, wq_ref, kv_ref, wo_ref, x2_ref, h3_ref):
    ndims = (((1,), (1,)), ((), ()))
    mem_w = MEM_HEADS * HEAD_DIM
    tm = x_ref.shape[0]
    nsub = CROSS_SUBTILES if tm % (16 * CROSS_SUBTILES) == 0 else 1
    subs = [slice(t * (tm // nsub), (t + 1) * (tm // nsub)) for t in range(nsub)]
    head_cols = [slice(h * HEAD_DIM, (h + 1) * HEAD_DIM) for h in range(MEM_HEADS)]

    hns = []
    for rows in subs:
        x = x_ref[rows, :]
        var = jnp.mean(x * x, axis=-1, keepdims=True)
        hns.append((x * lax.rsqrt(var + RMS_EPS) * gc_ref[...]).astype(BF16))
    qs = [jnp.dot(hn, wq_ref[...], preferred_element_type=F32).astype(BF16) for hn in hns]
    scores = [[lax.dot_general(q[:, c], kv_ref[:, c], ndims, preferred_element_type=F32) for c in head_cols]
              for q in qs]
    probs = []
    for sub_scores in scores:
        sub_probs = []
        for s in sub_scores:
            s = s * (HEAD_DIM ** -0.5)
            p = jnp.exp(s - jnp.max(s, axis=-1, keepdims=True))
            sub_probs.append((p.astype(BF16), 1.0 / jnp.sum(p, axis=-1, keepdims=True)))
        probs.append(sub_probs)
    os = []
    for sub_probs in probs:
        outs = []
        for h, (p, inv) in enumerate(sub_probs):
            v = kv_ref[:, mem_w + h * HEAD_DIM:mem_w + (h + 1) * HEAD_DIM]
            outs.append((jnp.dot(p, v, preferred_element_type=F32) * inv).astype(BF16))
        os.append(jnp.concatenate(outs, axis=1))
    x2s = [jnp.dot(o, wo_ref[...], preferred_element_type=F32) + x_ref[rows, :] for o, rows in zip(os, subs)]
    for x2, rows in zip(x2s, subs):
        x2_ref[rows, :] = x2
        var2 = jnp.mean(x2 * x2, axis=-1, keepdims=True)
        h3_ref[rows, :] = (x2 * lax.rsqrt(var2 + RMS_EPS) * gm_ref[...]).astype(h3_ref.dtype)


def _cross_block(x, g_cross, g_mlp, w_cq, kvm, w_co, mem_tokens, seq):
    n, d = x.shape
    mem_w = w_cq.shape[1]
    tm = _pick(seq, (512, 256, 128, 64, 32, 16, 8))
    row_spec = pl.BlockSpec((tm, d), lambda i: (i, 0))
    gain_spec = pl.BlockSpec((1, d), lambda i: (0, 0))
    resident = pl.Buffered(1)
    return pl.pallas_call(
        _cross_kernel,
        grid=(n // tm,),
        in_specs=[row_spec, gain_spec, gain_spec,
                  pl.BlockSpec((d, mem_w), lambda i: (0, 0), pipeline_mode=resident),
                  pl.BlockSpec((mem_tokens, 2 * mem_w), lambda i: (i // (seq // tm), 0)),
                  pl.BlockSpec((mem_w, d), lambda i: (0, 0), pipeline_mode=resident)],
        out_specs=[row_spec, row_spec],
        out_shape=[jax.ShapeDtypeStruct((n, d), F32), jax.ShapeDtypeStruct((n, d), BF16)],
        compiler_params=_params(("parallel",)),
        name="cross_block",
    )(x, g_cross.reshape(1, d).astype(F32), g_mlp.reshape(1, d).astype(F32), w_cq, kvm, w_co)


def _outproj_kernel(a1_ref, a2_ref, b1_ref, b2_ref, r_ref, o_ref):
    acc = jnp.dot(a1_ref[...], b1_ref[...], preferred_element_type=F32)
    acc = acc + jnp.dot(a2_ref[...], b2_ref[...], preferred_element_type=F32)
    o_ref[...] = acc + r_ref[...]


def _outproj(y_ret, y_att, w_out, x):
    m, k1 = y_ret.shape
    _, k2 = y_att.shape
    assert k1 % k2 == 0
    n = w_out.shape[1]
    tm = _pick(m, ROW_TILES)
    tn = _pick(n, COL_TILES)
    return pl.pallas_call(
        _outproj_kernel,
        grid=(m // tm, n // tn),
        in_specs=[pl.BlockSpec((tm, k1), lambda i, j: (i, 0)),
                  pl.BlockSpec((tm, k2), lambda i, j: (i, 0)),
                  pl.BlockSpec((k1, tn), lambda i, j: (0, j)),
                  pl.BlockSpec((k2, tn), lambda i, j: (k1 // k2, j)),
                  pl.BlockSpec((tm, tn), lambda i, j: (i, j))],
        out_specs=pl.BlockSpec((tm, tn), lambda i, j: (i, j)),
        out_shape=jax.ShapeDtypeStruct((m, n), F32),
        compiler_params=_params(("parallel", "arbitrary")),
        name="outproj",
    )(y_ret, y_att, w_out, w_out, x)


def _mm_kacc_kernel(*refs, side_arity):
    (a_ref, b_ref, r_ref), (o_ref,), sides = _split_side(refs, 3, 1, side_arity)
    kk = pl.program_id(2)

    def _dot():
        _run_sides(sides)
        return jnp.dot(a_ref[...], b_ref[...], preferred_element_type=F32)

    @pl.when(kk == 0)
    def _():
        o_ref[...] = r_ref[...] + _dot()

    @pl.when(kk != 0)
    def _():
        o_ref[...] += _dot()


def _matmul_kacc(a, b, residual, sides=()):
    m, k = a.shape
    _, n = b.shape
    tm = _pick(m, ROW_TILES)
    tn = _pick(n, COL_TILES)
    tk = _pick(k, (4096, 2048) + COL_TILES)
    nj, nk = n // tn, k // tk
    return _call_with_side(
        _mm_kacc_kernel, "matmul_kacc", (m // tm, nj, nk), ("parallel", "arbitrary", "arbitrary"),
        [pl.BlockSpec((tm, tk), lambda i, j, kk: (i, kk)), pl.BlockSpec((tk, tn), lambda i, j, kk: (kk, j)),
         pl.BlockSpec((tm, tn), lambda i, j, kk: (i, j))],
        [a, b, residual], pl.BlockSpec((tm, tn), lambda i, j, kk: (i, j)),
        jax.ShapeDtypeStruct((m, n), F32),
        sides, (m // tm) * nj * nk, lambda i, j, kk: (i * nj + j) * nk + kk)


def _rope_tables(seq):
    half = HEAD_DIM // 2
    inv = ROPE_BASE ** (-jnp.arange(half, dtype=F32) / half)
    ang = jnp.arange(seq, dtype=F32)[:, None] * inv[None, :]
    cos = jnp.cos(ang)
    sin = jnp.sin(ang)
    return jnp.concatenate([cos, cos], axis=-1), jnp.concatenate([-sin, sin], axis=-1)


INPROJ_CAST = ("w_mlp_in", "w_out", "w_cq", "w_ckv")


def _layer(x, h, mem, w, l, bias, rope, seq, mem_tokens, cast, mlp_in_sides=(), mlp_out_sides=()):
    ret_heads = w["dec_f"][l].shape[0]
    att_heads = w["sink"][l].shape[0]
    ret_w = ret_heads * HEAD_DIM
    first_group = cast is None
    proj, outs = _inproj(h, w["w_in"][l], *rope, ret_w, seq,
                         sides=[_SideJob(w[name][l]) for name in INPROJ_CAST] if first_group else ())
    if first_group:
        cast = dict(zip(INPROJ_CAST, outs))
        mlp_in_sides = [_SideJob(w["w_mlp_out"][l])] + list(mlp_in_sides)
    y_ret = _retention(proj, w["dec_f"][l], w["dec_b"][l], ret_heads, seq)
    y_att = _swa(proj, w["sink"][l], bias, ret_w, att_heads, seq)
    x = _outproj(y_ret, y_att, cast["w_out"], x)

    hm = _rmsnorm(mem, w["norm_mem"][l], BF16)
    kvm, _ = _matmul(hm, cast["w_ckv"], BF16)
    x, h = _cross_block(x, w["norm_cross"][l], w["norm_mlp"][l], cast["w_cq"], kvm, w["w_co"][l],
                        mem_tokens, seq)

    a, in_outs = _matmul(h, cast["w_mlp_in"], BF16, epilogue="relu2", sides=mlp_in_sides)
    if first_group:
        cast["w_mlp_out"], in_outs = in_outs[0], in_outs[1:]
    x, out_outs = _matmul_kacc(a, cast["w_mlp_out"], x, sides=mlp_out_sides)
    return x, cast, in_outs, out_outs


def kernel(x_prompt, x_sample, mem_prompt, mem_sample, norm_mix, w_in, ret_decay_f, ret_decay_b, attn_sink,
           rel_bias, w_out, norm_cross, norm_mem, w_cq, w_ckv, w_co, norm_mlp, w_mlp_in, w_mlp_out, norm_final):
    depth = w_in.shape[0]
    w = dict(norm_mix=norm_mix, w_in=w_in.astype(BF16), dec_f=ret_decay_f, dec_b=ret_decay_b, sink=attn_sink,
             w_out=w_out, norm_cross=norm_cross, norm_mem=norm_mem, w_cq=w_cq, w_ckv=w_ckv,
             w_co=w_co.astype(BF16), norm_mlp=norm_mlp, w_mlp_in=w_mlp_in, w_mlp_out=w_mlp_out)
    bias = _band_bias(rel_bias)
    d = x_prompt.shape[-1]
    mem_tokens = mem_prompt.shape[1]
    seq_a, seq_b = x_prompt.shape[1], x_sample.shape[1]
    xa, xb = x_prompt.reshape(-1, d), x_sample.reshape(-1, d)
    ma, mb = mem_prompt.reshape(-1, d), mem_sample.reshape(-1, d)
    rope = _rope_tables(max(seq_a, seq_b))

    casts = []
    hb = None
    for l in range(depth):
        ha = _rmsnorm(xa, norm_mix[l], BF16)
        sides = [_SideJob(xb, norm_mix[0], BF16)] if l == 0 else []
        xa, cast, _, outs = _layer(xa, ha, ma, w, l, bias, rope, seq_a, mem_tokens, None, mlp_out_sides=sides)
        casts.append(cast)
        if l == 0:
            hb = outs[0]
    ya = None
    for l in range(depth):
        if l > 0:
            hb = _rmsnorm(xb, norm_mix[l], BF16)
        sides = [_SideJob(xa, norm_final, F32)] if l == 0 else []
        xb, _, outs, _ = _layer(xb, hb, mb, w, l, bias, rope, seq_b, mem_tokens, casts[l], mlp_in_sides=sides)
        if l == 0:
            ya = outs[0]
    yb = _rmsnorm(xb, norm_final, F32)
    return ya.reshape(x_prompt.shape), yb.reshape(x_sample.shape)
```

```python
import functools
import math

import numpy as np
import jax
import jax.numpy as jnp
from jax import lax
from jax.experimental import pallas as pl
from jax.experimental.pallas import tpu as pltpu

HEAD_DIM = 128
RET_CHUNK = 128
WINDOW = 128
BLOCK = 128
N_BUCKETS = 32
MAX_DISTANCE = 128
MEM_HEADS = 4
GQA_GROUP = 4
ROPE_BASE = 10000.0
RMS_EPS = 1e-6
NEG_INF = -1e30

VMEM_LIMIT_BYTES = 60 * 1024 * 1024

BF16 = jnp.bfloat16
F32 = jnp.float32

ROW_TILES = (1024, 512, 256, 128, 64, 32, 16, 8)
COL_TILES = (1024, 512, 256, 128)


def _pick(n, prefs):
    for p in prefs:
        if n % p == 0:
            return p
    raise ValueError(f"no tile in {prefs} divides {n}")


def _params(sem):
    return pltpu.CompilerParams(dimension_semantics=sem, vmem_limit_bytes=VMEM_LIMIT_BYTES)


def _rmsnorm_kernel(x_ref, g_ref, o_ref):
    x = x_ref[...]
    var = jnp.mean(x * x, axis=-1, keepdims=True)
    o_ref[...] = (x * lax.rsqrt(var + RMS_EPS) * g_ref[...]).astype(o_ref.dtype)


def _rmsnorm(x, g, out_dtype):
    n, d = x.shape
    tm = _pick(n, (256, 128, 64, 32, 16, 8))
    return pl.pallas_call(
        _rmsnorm_kernel,
        grid=(n // tm,),
        in_specs=[pl.BlockSpec((tm, d), lambda i: (i, 0)),
                  pl.BlockSpec((1, d), lambda i: (0, 0))],
        out_specs=pl.BlockSpec((tm, d), lambda i: (i, 0)),
        out_shape=jax.ShapeDtypeStruct((n, d), out_dtype),
        compiler_params=_params(("parallel",)),
        name="rmsnorm",
    )(x, g.reshape(1, d).astype(F32))


class _SideJob:
    def __init__(self, src, gain=None, out_dtype=BF16):
        self.src = src
        self.gain = gain
        self.out_dtype = out_dtype

    def plan(self, nblocks, block_of_step):
        rows, cols = self.src.shape
        if rows % nblocks or (rows // nblocks) % 16:
            return None
        spec = pl.BlockSpec((rows // nblocks, cols), lambda *ids: (block_of_step(*ids), 0))
        in_specs, args = [spec], [self.src]
        if self.gain is not None:
            in_specs.append(pl.BlockSpec((1, cols), lambda *ids: (0, 0)))
            args.append(self.gain.reshape(1, cols).astype(F32))
        return in_specs, args, spec, jax.ShapeDtypeStruct(self.src.shape, self.out_dtype)

    def standalone(self):
        if self.gain is None:
            return self.src.astype(self.out_dtype)
        return _rmsnorm(self.src, self.gain, self.out_dtype)


def _side_body(in_refs, out_ref):
    if len(in_refs) == 1:
        out_ref[...] = in_refs[0][...].astype(out_ref.dtype)
    else:
        _rmsnorm_kernel(in_refs[0], in_refs[1], out_ref)


def _split_side(refs, n_main_in, n_main_out, side_arity):
    main_in = refs[:n_main_in]
    pos = n_main_in
    side_ins = []
    for n in side_arity:
        side_ins.append(refs[pos:pos + n])
        pos += n
    main_out = refs[pos:pos + n_main_out]
    side_outs = refs[pos + n_main_out:]
    return main_in, main_out, list(zip(side_ins, side_outs))


def _run_sides(sides):
    for in_refs, out_ref in sides:
        _side_body(in_refs, out_ref)


def _call_with_side(kernel_fn, name, grid, sem, in_specs, args, out_spec, out_shape, sides, nblocks,
                    block_of_step):
    in_specs, args = list(in_specs), list(args)
    out_specs, out_shapes, arity, hosted = [out_spec], [out_shape], [], []
    for job in sides:
        plan = job.plan(nblocks, block_of_step)
        hosted.append(plan is not None)
        if plan is not None:
            s_in_specs, s_args, s_out_spec, s_out_shape = plan
            in_specs += s_in_specs
            args += s_args
            out_specs.append(s_out_spec)
            out_shapes.append(s_out_shape)
            arity.append(len(s_in_specs))
    outs = pl.pallas_call(
        functools.partial(kernel_fn, side_arity=tuple(arity)), grid=grid, in_specs=in_specs,
        out_specs=out_specs, out_shape=out_shapes, compiler_params=_params(sem),
        name=name + ("_side" if arity else ""),
    )(*args)
    hosted_outs = iter(outs[1:])
    return outs[0], [next(hosted_outs) if h else job.standalone() for job, h in zip(sides, hosted)]


def _mm_kernel(*refs, epilogue, side_arity):
    (a_ref, b_ref), (o_ref,), sides = _split_side(refs, 2, 1, side_arity)
    acc = jnp.dot(a_ref[...], b_ref[...], preferred_element_type=F32)
    if epilogue == "relu2":
        acc = jnp.maximum(acc, 0.0)
        acc = acc * acc
    o_ref[...] = acc.astype(o_ref.dtype)
    _run_sides(sides)


def _matmul(a, b, out_dtype, epilogue="none", sides=()):
    m, k = a.shape
    _, n = b.shape
    tm = _pick(m, ROW_TILES)
    tn = _pick(n, COL_TILES)
    nj = n // tn
    return _call_with_side(
        functools.partial(_mm_kernel, epilogue=epilogue), "matmul_" + epilogue,
        (m // tm, nj), ("parallel", "arbitrary"),
        [pl.BlockSpec((tm, k), lambda i, j: (i, 0)), pl.BlockSpec((k, tn), lambda i, j: (0, j))],
        [a, b], pl.BlockSpec((tm, tn), lambda i, j: (i, j)), jax.ShapeDtypeStruct((m, n), out_dtype),
        sides, (m // tm) * nj, lambda i, j: i * nj + j)


def _inproj_kernel(*refs, q_tiles, k_tiles, side_arity):
    (a_ref, b_ref, cos_ref, sin_ref), (o_ref,), sides = _split_side(refs, 4, 1, side_arity)
    j = pl.program_id(1)

    def _dot():
        _run_sides(sides)
        return jnp.dot(a_ref[...], b_ref[...], preferred_element_type=F32)

    def _store_rope(scale):
        acc = _dot()
        cos = cos_ref[...]
        sin = sin_ref[...]
        if scale != 1.0:
            cos = cos * scale
            sin = sin * scale
        for h in range(acc.shape[1] // HEAD_DIM):
            cols = slice(h * HEAD_DIM, (h + 1) * HEAD_DIM)
            x = acc[:, cols]
            o_ref[:, cols] = (x * cos + pltpu.roll(x, HEAD_DIM // 2, 1) * sin).astype(o_ref.dtype)

    @pl.when(j < q_tiles)
    def _():
        _store_rope(1.0)

    @pl.when(jnp.logical_and(j >= q_tiles, j < q_tiles + k_tiles))
    def _():
        _store_rope(HEAD_DIM ** -0.5)

    @pl.when(j >= q_tiles + k_tiles)
    def _():
        o_ref[...] = _dot().astype(o_ref.dtype)


def _inproj(h, w_in, cos_t, sin_t, ret_w, seq, sides=()):
    m, k = h.shape
    _, n = w_in.shape
    tn = _pick(math.gcd(n, ret_w), COL_TILES)
    tm = _pick(seq, ROW_TILES)
    nj = n // tn
    nj_side = 1 << (nj.bit_length() - 1)
    pos_spec = pl.BlockSpec((tm, HEAD_DIM), lambda i, j: (i % (seq // tm), 0))
    return _call_with_side(
        functools.partial(_inproj_kernel, q_tiles=ret_w // tn, k_tiles=ret_w // tn), "inproj_rope",
        (m // tm, nj), ("parallel", "arbitrary"),
        [pl.BlockSpec((tm, k), lambda i, j: (i, 0)), pl.BlockSpec((k, tn), lambda i, j: (0, j)),
         pos_spec, pos_spec],
        [h, w_in, cos_t, sin_t], pl.BlockSpec((tm, tn), lambda i, j: (i, j)),
        jax.ShapeDtypeStruct((m, n), BF16),
        sides, (m // tm) * nj_side, lambda i, j: i * nj_side + jnp.minimum(j, nj_side - 1))


RET_MAX_UNROLL = 16
RET_HEADS_PER_STEP = 2


def _ret_kernel(decf_ref, decb_ref, q_ref, k_ref, v_ref, g_ref, o_ref, st_scr, tab_scr, *, nc):
    for head in range(q_ref.shape[1] // HEAD_DIM):
        _ret_head(decf_ref, decb_ref, q_ref, k_ref, v_ref, g_ref, o_ref, st_scr, tab_scr, head, nc)


def _ret_head(decf_ref, decb_ref, q_ref, k_ref, v_ref, g_ref, o_ref, st_scr, tab_scr, head, nc):
    C = RET_CHUNK
    RET_UNROLL = math.gcd(nc, RET_MAX_UNROLL)
    hc = slice(head * HEAD_DIM, (head + 1) * HEAD_DIM)
    lgf = -jnp.exp(jnp.broadcast_to(decf_ref[head, 0:1, :], (C, C)))
    lgb = -jnp.exp(jnp.broadcast_to(decb_ref[head, 0:1, :], (C, C)))
    row = lax.broadcasted_iota(jnp.int32, (C, C), 0).astype(F32)
    col = lax.broadcasted_iota(jnp.int32, (C, C), 1).astype(F32)
    diff = row - col
    tab_scr[0] = jnp.where(diff >= 0, jnp.exp(lgf * jnp.maximum(diff, 0.0)),
                           jnp.exp(lgb * jnp.maximum(-diff, 0.0)))
    tab_scr[1] = jnp.exp(lgf * (row + 1.0))
    tab_scr[2] = jnp.exp(lgb * (C - row))
    tab_scr[3] = jnp.exp(lgf * (C - 1.0 - row))
    tab_scr[4] = jnp.exp(lgb * row)
    cdf = jnp.exp(lgf * C)
    cdb = jnp.exp(lgb * C)

    tdims = (((0,), (0,)), ((), ()))
    ndims = (((1,), (1,)), ((), ()))

    def chunk_rows(c):
        return pl.ds(pl.multiple_of(c * C, C), C)

    def scan_body(t, carry):
        sf, sb = carry
        cf = t
        cb = nc - 1 - t
        st_scr[cf, 0:C, :] = sf.astype(BF16)
        st_scr[cb, C:2 * C, :] = sb.astype(BF16)
        rf = chunk_rows(cf)
        rb = chunk_rows(cb)
        kzf = (k_ref[rf, hc].astype(F32) * tab_scr[3]).astype(BF16)
        kzb = (k_ref[rb, hc].astype(F32) * tab_scr[4]).astype(BF16)
        sf = sf * cdf + lax.dot_general(kzf, v_ref[rf, hc], tdims, preferred_element_type=F32)
        sb = sb * cdb + lax.dot_general(kzb, v_ref[rb, hc], tdims, preferred_element_type=F32)
        return sf, sb

    zero = jnp.zeros((C, C), F32)
    lax.fori_loop(0, nc, scan_body, (zero, zero), unroll=RET_UNROLL)

    def out_body(t, carry):
        cs = [t * RET_UNROLL + i for i in range(RET_UNROLL)]
        rows = [chunk_rows(c) for c in cs]
        qs = [q_ref[r, hc] for r in rows]
        ss = [lax.dot_general(q, k_ref[r, hc], ndims, preferred_element_type=F32) for q, r in zip(qs, rows)]
        atts = [(s * tab_scr[0]).astype(BF16) for s in ss]
        lhss = []
        for q, att in zip(qs, atts):
            qf = q.astype(F32)
            lhss.append(jnp.concatenate([att, (qf * tab_scr[1]).astype(BF16), (qf * tab_scr[2]).astype(BF16)],
                                        axis=1))
        ys = [jnp.dot(lhs, jnp.concatenate([v_ref[r, hc], st_scr[c]], axis=0), preferred_element_type=F32)
              for lhs, r, c in zip(lhss, rows, cs)]
        for y, r in zip(ys, rows):
            var = jnp.mean(y * y, axis=-1, keepdims=True)
            half_g = 0.5 * g_ref[r, hc].astype(F32)
            gate = half_g + half_g * jnp.tanh(half_g)
            o_ref[r, hc] = (y * lax.rsqrt(var + RMS_EPS) * gate).astype(o_ref.dtype)
        return carry

    lax.fori_loop(0, nc // RET_UNROLL, out_body, 0)


def _retention(proj, dec_f, dec_b, ret_heads, seq):
    n = proj.shape[0]
    nc = seq // RET_CHUNK
    dec_f = jnp.broadcast_to(dec_f.astype(F32).reshape(ret_heads, 1, 1), (ret_heads, 8, HEAD_DIM))
    dec_b = jnp.broadcast_to(dec_b.astype(F32).reshape(ret_heads, 1, 1), (ret_heads, 8, HEAD_DIM))
    hps = math.gcd(ret_heads, RET_HEADS_PER_STEP)
    groups = ret_heads // hps
    dec_spec = pl.BlockSpec((hps, 8, HEAD_DIM), lambda b, h: (h, 0, 0))

    def col_spec(base_group):
        return pl.BlockSpec((seq, hps * HEAD_DIM), lambda b, h: (b, base_group + h))

    return pl.pallas_call(
        functools.partial(_ret_kernel, nc=nc),
        grid=(n // seq, groups),
        in_specs=[dec_spec, dec_spec, col_spec(0), col_spec(groups), col_spec(2 * groups),
                  col_spec(3 * groups)],
        out_specs=pl.BlockSpec((seq, hps * HEAD_DIM), lambda b, h: (b, h)),
        out_shape=jax.ShapeDtypeStruct((n, ret_heads * HEAD_DIM), BF16),
        scratch_shapes=[pltpu.VMEM((nc, 2 * RET_CHUNK, RET_CHUNK), BF16),
                        pltpu.VMEM((5, RET_CHUNK, RET_CHUNK), F32)],
        compiler_params=_params(("parallel", "arbitrary")),
        name="retention",
    )(dec_f, dec_b, proj, proj, proj, proj)


SWA_MAX_QBLOCKS = 8


def _swa_kernel(sink_ref, q_ref, k_ref, v_ref, bias_ref, o_ref, *, nblk):
    B = BLOCK
    SWA_QBLOCKS = q_ref.shape[0] // B
    kvh = pl.program_id(1)
    step = pl.program_id(2)
    ndims = (((1,), (1,)), ((), ()))
    scale = HEAD_DIM ** -0.5
    exp2_coef = scale * math.log2(math.e)
    bands = []
    scores = []
    for qb in range(SWA_QBLOCKS):
        n = step * SWA_QBLOCKS + qb
        variant = jnp.where(n > 0, 0, 1) + jnp.where(n < nblk - 1, 0, 2)
        rp = pl.ds(pl.multiple_of(jnp.maximum(n - 1, 0) * B, B), B)
        rc = pl.ds(pl.multiple_of(n * B, B), B)
        rn = pl.ds(pl.multiple_of(jnp.minimum(n + 1, nblk - 1) * B, B), B)
        kband = jnp.concatenate([k_ref[rp, :], k_ref[rc, :], k_ref[rn, :]], axis=0)
        vband = jnp.concatenate([v_ref[rp, :], v_ref[rc, :], v_ref[rn, :]], axis=0)
        bands.append((vband, variant))
        for g in range(GQA_GROUP):
            q = q_ref[qb * B:(qb + 1) * B, g * HEAD_DIM:(g + 1) * HEAD_DIM]
            scores.append(lax.dot_general(q, kband, ndims, preferred_element_type=F32))
    probs = []
    for idx, s in enumerate(scores):
        qb, g = divmod(idx, GQA_GROUP)
        sink = sink_ref[kvh * GQA_GROUP + g] * (1.0 / scale)
        t = s + bias_ref[bands[qb][1], g]
        mx = jnp.maximum(jnp.max(t, axis=-1, keepdims=True), sink)
        p = jnp.exp2((t - mx) * exp2_coef)
        denom = jnp.sum(p, axis=-1, keepdims=True) + jnp.exp2((sink - mx) * exp2_coef)
        probs.append((p.astype(BF16), 1.0 / denom))
    for idx, (p, inv) in enumerate(probs):
        qb, g = divmod(idx, GQA_GROUP)
        o = jnp.dot(p, bands[qb][0], preferred_element_type=F32) * inv
        o_ref[qb * B:(qb + 1) * B, g * HEAD_DIM:(g + 1) * HEAD_DIM] = o.astype(o_ref.dtype)


def _swa(proj, sink, bias, ret_w, att_heads, seq):
    n = proj.shape[0]
    kv_heads = att_heads // GQA_GROUP
    nblk = seq // BLOCK
    qblocks = math.gcd(nblk, SWA_MAX_QBLOCKS)
    steps = nblk // qblocks
    qrows = qblocks * BLOCK
    gw = GQA_GROUP * HEAD_DIM
    q_base = 4 * ret_w // gw
    k_base = (4 * ret_w + att_heads * HEAD_DIM) // HEAD_DIM
    v_base = k_base + kv_heads
    return pl.pallas_call(
        functools.partial(_swa_kernel, nblk=nblk),
        grid=(n // seq, kv_heads, steps),
        in_specs=[pl.BlockSpec(memory_space=pltpu.SMEM),
                  pl.BlockSpec((qrows, gw), lambda b, kv, i: (b * steps + i, q_base + kv)),
                  pl.BlockSpec((seq, HEAD_DIM), lambda b, kv, i: (b, k_base + kv)),
                  pl.BlockSpec((seq, HEAD_DIM), lambda b, kv, i: (b, v_base + kv)),
                  pl.BlockSpec((4, GQA_GROUP, BLOCK, 3 * BLOCK), lambda b, kv, i: (0, kv, 0, 0))],
        out_specs=pl.BlockSpec((qrows, gw), lambda b, kv, i: (b * steps + i, kv)),
        out_shape=jax.ShapeDtypeStruct((n, att_heads * HEAD_DIM), BF16),
        compiler_params=_params(("parallel", "arbitrary", "arbitrary")),
        name="swa",
    )(sink.astype(F32), proj, proj, proj, bias)


def _t5_buckets(rel):
    nb = N_BUCKETS // 2
    max_exact = nb // 2
    base = np.where(rel > 0, nb, 0)
    n = np.abs(rel)
    large = max_exact + (np.log(np.maximum(n, 1) / max_exact) / np.log(MAX_DISTANCE / max_exact)
                         * (nb - max_exact)).astype(np.int32)
    large = np.minimum(large, nb - 1)
    return (base + np.where(n < max_exact, n, large)).astype(np.int32)


def _band_bias(rel_bias):
    qi = np.arange(BLOCK)[:, None]
    kj = np.arange(3 * BLOCK)[None, :] - BLOCK
    rel = kj - qi
    buckets = jnp.asarray(_t5_buckets(rel).reshape(-1, 1))
    onehot = (buckets == jnp.arange(N_BUCKETS, dtype=jnp.int32)[None, :]).astype(F32)
    table = jnp.dot(onehot, rel_bias.astype(F32), precision=lax.Precision.HIGHEST)
    table = jnp.transpose(table).reshape(rel_bias.shape[1], BLOCK, 3 * BLOCK) * (HEAD_DIM ** 0.5)
    in_window = np.abs(rel) <= WINDOW
    variants = []
    for v in range(4):
        ok = in_window & ((kj >= 0) | (v & 1 == 0)) & ((kj < BLOCK) | (v & 2 == 0))
        variants.append(jnp.where(jnp.asarray(ok)[None], table, NEG_INF))
    return jnp.stack(variants)


CROSS_SUBTILES = 2


def _cross_kernel(x_ref, gc_ref, gm_ref, wq_ref, kv_ref, wo_ref, x2_ref, h3_ref):
    ndims = (((1,), (1,)), ((), ()))
    mem_w = MEM_HEADS * HEAD_DIM
    tm = x_ref.shape[0]
    nsub = CROSS_SUBTILES if tm % (16 * CROSS_SUBTILES) == 0 else 1
    subs = [slice(t * (tm // nsub), (t + 1) * (tm // nsub)) for t in range(nsub)]
    head_cols = [slice(h * HEAD_DIM, (h + 1) * HEAD_DIM) for h in range(MEM_HEADS)]

    hns = []
    for rows in subs:
        x = x_ref[rows, :]
        var = jnp.mean(x * x, axis=-1, keepdims=True)
        hns.append((x * lax.rsqrt(var + RMS_EPS) * gc_ref[...]).astype(BF16))
    qs = [jnp.dot(hn, wq_ref[...], preferred_element_type=F32).astype(BF16) for hn in hns]
    scores = [[lax.dot_general(q[:, c], kv_ref[:, c], ndims, preferred_element_type=F32) for c in head_cols]
              for q in qs]
    probs = []
    for sub_scores in scores:
        sub_probs = []
        for s in sub_scores:
            s = s * (HEAD_DIM ** -0.5)
            p = jnp.exp(s - jnp.max(s, axis=-1, keepdims=True))
            sub_probs.append((p.astype(BF16), 1.0 / jnp.sum(p, axis=-1, keepdims=True)))
        probs.append(sub_probs)
    os = []
    for sub_probs in probs:
        outs = []
        for h, (p, inv) in enumerate(sub_probs):
            v = kv_ref[:, mem_w + h * HEAD_DIM:mem_w + (h + 1) * HEAD_DIM]
            outs.append((jnp.dot(p, v, preferred_element_type=F32) * inv).astype(BF16))
        os.append(jnp.concatenate(outs, axis=1))
    x2s = [jnp.dot(o, wo_ref[...], preferred_element_type=F32) + x_ref[rows, :] for o, rows in zip(os, subs)]
    for x2, rows in zip(x2s, subs):
        x2_ref[rows, :] = x2
        var2 = jnp.mean(x2 * x2, axis=-1, keepdims=True)
        h3_ref[rows, :] = (x2 * lax.rsqrt(var2 + RMS_EPS) * gm_ref[...]).astype(h3_ref.dtype)


def _cross_block(x, g_cross, g_mlp, w_cq, kvm, w_co, mem_tokens, seq):
    n, d = x.shape
    mem_w = w_cq.shape[1]
    tm = _pick(seq, (512, 256, 128, 64, 32, 16, 8))
    row_spec = pl.BlockSpec((tm, d), lambda i: (i, 0))
    gain_spec = pl.BlockSpec((1, d), lambda i: (0, 0))
    resident = pl.Buffered(1)
    return pl.pallas_call(
        _cross_kernel,
        grid=(n // tm,),
        in_specs=[row_spec, gain_spec, gain_spec,
                  pl.BlockSpec((d, mem_w), lambda i: (0, 0), pipeline_mode=resident),
                  pl.BlockSpec((mem_tokens, 2 * mem_w), lambda i: (i // (seq // tm), 0)),
                  pl.BlockSpec((mem_w, d), lambda i: (0, 0), pipeline_mode=resident)],
        out_specs=[row_spec, row_spec],
        out_shape=[jax.ShapeDtypeStruct((n, d), F32), jax.ShapeDtypeStruct((n, d), BF16)],
        compiler_params=_params(("parallel",)),
        name="cross_block",
    )(x, g_cross.reshape(1, d).astype(F32), g_mlp.reshape(1, d).astype(F32), w_cq, kvm, w_co)


def _outproj_kernel(a1_ref, a2_ref, b1_ref, b2_ref, r_ref, o_ref):
    acc = jnp.dot(a1_ref[...], b1_ref[...], preferred_element_type=F32)
    acc = acc + jnp.dot(a2_ref[...], b2_ref[...], preferred_element_type=F32)
    o_ref[...] = acc + r_ref[...]


def _outproj(y_ret, y_att, w_out, x):
    m, k1 = y_ret.shape
    _, k2 = y_att.shape
    assert k1 % k2 == 0
    n = w_out.shape[1]
    tm = _pick(m, ROW_TILES)
    tn = _pick(n, COL_TILES)
    return pl.pallas_call(
        _outproj_kernel,
        grid=(m // tm, n // tn),
        in_specs=[pl.BlockSpec((tm, k1), lambda i, j: (i, 0)),
                  pl.BlockSpec((tm, k2), lambda i, j: (i, 0)),
                  pl.BlockSpec((k1, tn), lambda i, j: (0, j)),
                  pl.BlockSpec((k2, tn), lambda i, j: (k1 // k2, j)),
                  pl.BlockSpec((tm, tn), lambda i, j: (i, j))],
        out_specs=pl.BlockSpec((tm, tn), lambda i, j: (i, j)),
        out_shape=jax.ShapeDtypeStruct((m, n), F32),
        compiler_params=_params(("parallel", "arbitrary")),
        name="outproj",
    )(y_ret, y_att, w_out, w_out, x)


def _mm_kacc_kernel(*refs, side_arity):
    (a_ref, b_ref, r_ref), (o_ref,), sides = _split_side(refs, 3, 1, side_arity)
    kk = pl.program_id(2)

    def _dot():
        _run_sides(sides)
        return jnp.dot(a_ref[...], b_ref[...], preferred_element_type=F32)

    @pl.when(kk == 0)
    def _():
        o_ref[...] = r_ref[...] + _dot()

    @pl.when(kk != 0)
    def _():
        o_ref[...] += _dot()


def _matmul_kacc(a, b, residual, sides=()):
    m, k = a.shape
    _, n = b.shape
    tm = _pick(m, ROW_TILES)
    tn = _pick(n, COL_TILES)
    tk = _pick(k, (4096, 2048) + COL_TILES)
    nj, nk = n // tn, k // tk
    return _call_with_side(
        _mm_kacc_kernel, "matmul_kacc", (m // tm, nj, nk), ("parallel", "arbitrary", "arbitrary"),
        [pl.BlockSpec((tm, tk), lambda i, j, kk: (i, kk)), pl.BlockSpec((tk, tn), lambda i, j, kk: (kk, j)),
         pl.BlockSpec((tm, tn), lambda i, j, kk: (i, j))],
        [a, b, residual], pl.BlockSpec((tm, tn), lambda i, j, kk: (i, j)),
        jax.ShapeDtypeStruct((m, n), F32),
        sides, (m // tm) * nj * nk, lambda i, j, kk: (i * nj + j) * nk + kk)


def _rope_tables(seq):
    half = HEAD_DIM // 2
    inv = ROPE_BASE ** (-jnp.arange(half, dtype=F32) / half)
    ang = jnp.arange(seq, dtype=F32)[:, None] * inv[None, :]
    cos = jnp.cos(ang)
    sin = jnp.sin(ang)
    return jnp.concatenate([cos, cos], axis=-1), jnp.concatenate([-sin, sin], axis=-1)


INPROJ_CAST = ("w_mlp_in", "w_out", "w_cq", "w_ckv")


def _layer(x, h, mem, w, l, bias, rope, seq, mem_tokens, cast, mlp_in_sides=(), mlp_out_sides=()):
    ret_heads = w["dec_f"][l].shape[0]
    att_heads = w["sink"][l].shape[0]
    ret_w = ret_heads * HEAD_DIM
    first_group = cast is None
    proj, outs = _inproj(h, w["w_in"][l], *rope, ret_w, seq,
                         sides=[_SideJob(w[name][l]) for name in INPROJ_CAST] if first_group else ())
    if first_group:
        cast = dict(zip(INPROJ_CAST, outs))
        mlp_in_sides = [_SideJob(w["w_mlp_out"][l])] + list(mlp_in_sides)
    y_ret = _retention(proj, w["dec_f"][l], w["dec_b"][l], ret_heads, seq)
    y_att = _swa(proj, w["sink"][l], bias, ret_w, att_heads, seq)
    x = _outproj(y_ret, y_att, cast["w_out"], x)

    hm = _rmsnorm(mem, w["norm_mem"][l], BF16)
    kvm, _ = _matmul(hm, cast["w_ckv"], BF16)
    x, h = _cross_block(x, w["norm_cross"][l], w["norm_mlp"][l], cast["w_cq"], kvm, w["w_co"][l],
                        mem_tokens, seq)

    a, in_outs = _matmul(h, cast["w_mlp_in"], BF16, epilogue="relu2", sides=mlp_in_sides)
    if first_group:
        cast["w_mlp_out"], in_outs = in_outs[0], in_outs[1:]
    x, out_outs = _matmul_kacc(a, cast["w_mlp_out"], x, sides=mlp_out_sides)
    return x, cast, in_outs, out_outs


def kernel(x_prompt, x_sample, mem_prompt, mem_sample, norm_mix, w_in, ret_decay_f, ret_decay_b, attn_sink,
           rel_bias, w_out, norm_cross, norm_mem, w_cq, w_ckv, w_co, norm_mlp, w_mlp_in, w_mlp_out, norm_final):
    depth = w_in.shape[0]
    w = dict(norm_mix=norm_mix, w_in=w_in.astype(BF16), dec_f=ret_decay_f, dec_b=ret_decay_b, sink=attn_sink,
             w_out=w_out, norm_cross=norm_cross, norm_mem=norm_mem, w_cq=w_cq, w_ckv=w_ckv,
             w_co=w_co.astype(BF16), norm_mlp=norm_mlp, w_mlp_in=w_mlp_in, w_mlp_out=w_mlp_out)
    bias = _band_bias(rel_bias)
    d = x_prompt.shape[-1]
    mem_tokens = mem_prompt.shape[1]
    seq_a, seq_b = x_prompt.shape[1], x_sample.shape[1]
    xa, xb = x_prompt.reshape(-1, d), x_sample.reshape(-1, d)
    ma, mb = mem_prompt.reshape(-1, d), mem_sample.reshape(-1, d)
    rope = _rope_tables(max(seq_a, seq_b))

    casts = []
    hb = None
    for l in range(depth):
        ha = _rmsnorm(xa, norm_mix[l], BF16)
        sides = [_SideJob(xb, norm_mix[0], BF16)] if l == 0 else []
        xa, cast, _, outs = _layer(xa, ha, ma, w, l, bias, rope, seq_a, mem_tokens, None, mlp_out_sides=sides)
        casts.append(cast)
        if l == 0:
            hb = outs[0]
    ya = None
    for l in range(depth):
        if l > 0:
            hb = _rmsnorm(xb, norm_mix[l], BF16)
        sides = [_SideJob(xa, norm_final, F32)] if l == 0 else []
        xb, _, outs, _ = _layer(xb, hb, mb, w, l, bias, rope, seq_b, mem_tokens, casts[l], mlp_in_sides=sides)
        if l == 0:
            ya = outs[0]
    yb = _rmsnorm(xb, norm_final, F32)
    return ya.reshape(x_prompt.shape), yb.reshape(x_sample.shape)
```

```python
import functools
import math

import numpy as np
import jax
import jax.numpy as jnp
from jax import lax
from jax.experimental import pallas as pl
from jax.experimental.pallas import tpu as pltpu

HEAD_DIM = 128
RET_CHUNK = 128
WINDOW = 128
BLOCK = 128
N_BUCKETS = 32
MAX_DISTANCE = 128
MEM_HEADS = 4
GQA_GROUP = 4
ROPE_BASE = 10000.0
RMS_EPS = 1e-6
NEG_INF = -1e30

VMEM_LIMIT_BYTES = 60 * 1024 * 1024

BF16 = jnp.bfloat16
F32 = jnp.float32

ROW_TILES = (1024, 512, 256, 128, 64, 32, 16, 8)
COL_TILES = (1024, 512, 256, 128)


def _pick(n, prefs):
    for p in prefs:
        if n % p == 0:
            return p
    raise ValueError(f"no tile in {prefs} divides {n}")


def _params(sem):
    return pltpu.CompilerParams(dimension_semantics=sem, vmem_limit_bytes=VMEM_LIMIT_BYTES)


def _rmsnorm_kernel(x_ref, g_ref, o_ref):
    x = x_ref[...]
    var = jnp.mean(x * x, axis=-1, keepdims=True)
    o_ref[...] = (x * lax.rsqrt(var + RMS_EPS) * g_ref[...]).astype(o_ref.dtype)


def _rmsnorm(x, g, out_dtype):
    n, d = x.shape
    tm = _pick(n, (512, 256, 128, 64, 32, 16, 8))
    return pl.pallas_call(
        _rmsnorm_kernel,
        grid=(n // tm,),
        in_specs=[pl.BlockSpec((tm, d), lambda i: (i, 0)),
                  pl.BlockSpec((1, d), lambda i: (0, 0))],
        out_specs=pl.BlockSpec((tm, d), lambda i: (i, 0)),
        out_shape=jax.ShapeDtypeStruct((n, d), out_dtype),
        compiler_params=_params(("parallel",)),
        name="rmsnorm",
    )(x, g.reshape(1, d).astype(F32))


class _SideJob:
    def __init__(self, src, gain=None, out_dtype=BF16):
        self.src = src
        self.gain = gain
        self.out_dtype = out_dtype

    def plan(self, nblocks, block_of_step):
        rows, cols = self.src.shape
        if rows % nblocks or (rows // nblocks) % 16:
            return None
        spec = pl.BlockSpec((rows // nblocks, cols), lambda *ids: (block_of_step(*ids), 0))
        in_specs, args = [spec], [self.src]
        if self.gain is not None:
            in_specs.append(pl.BlockSpec((1, cols), lambda *ids: (0, 0)))
            args.append(self.gain.reshape(1, cols).astype(F32))
        return in_specs, args, spec, jax.ShapeDtypeStruct(self.src.shape, self.out_dtype)

    def standalone(self):
        if self.gain is None:
            return self.src.astype(self.out_dtype)
        return _rmsnorm(self.src, self.gain, self.out_dtype)


def _side_body(in_refs, out_ref):
    if len(in_refs) == 1:
        out_ref[...] = in_refs[0][...].astype(out_ref.dtype)
    else:
        _rmsnorm_kernel(in_refs[0], in_refs[1], out_ref)


def _split_side(refs, n_main_in, n_main_out, side_arity):
    main_in = refs[:n_main_in]
    pos = n_main_in
    side_ins = []
    for n in side_arity:
        side_ins.append(refs[pos:pos + n])
        pos += n
    main_out = refs[pos:pos + n_main_out]
    side_outs = refs[pos + n_main_out:]
    return main_in, main_out, list(zip(side_ins, side_outs))


def _run_sides(sides):
    for in_refs, out_ref in sides:
        _side_body(in_refs, out_ref)


def _call_with_side(kernel_fn, name, grid, sem, in_specs, args, out_spec, out_shape, sides, nblocks,
                    block_of_step):
    in_specs, args = list(in_specs), list(args)
    out_specs, out_shapes, arity, hosted = [out_spec], [out_shape], [], []
    for job in sides:
        plan = job.plan(nblocks, block_of_step)
        hosted.append(plan is not None)
        if plan is not None:
            s_in_specs, s_args, s_out_spec, s_out_shape = plan
            in_specs += s_in_specs
            args += s_args
            out_specs.append(s_out_spec)
            out_shapes.append(s_out_shape)
            arity.append(len(s_in_specs))
    outs = pl.pallas_call(
        functools.partial(kernel_fn, side_arity=tuple(arity)), grid=grid, in_specs=in_specs,
        out_specs=out_specs, out_shape=out_shapes, compiler_params=_params(sem),
        name=name + ("_side" if arity else ""),
    )(*args)
    hosted_outs = iter(outs[1:])
    return outs[0], [next(hosted_outs) if h else job.standalone() for job, h in zip(sides, hosted)]


def _mm_kernel(*refs, epilogue, side_arity):
    (a_ref, b_ref), (o_ref,), sides = _split_side(refs, 2, 1, side_arity)
    acc = jnp.dot(a_ref[...], b_ref[...], preferred_element_type=F32)
    if epilogue == "relu2":
        acc = jnp.maximum(acc, 0.0)
        acc = acc * acc
    o_ref[...] = acc.astype(o_ref.dtype)
    _run_sides(sides)


def _matmul(a, b, out_dtype, epilogue="none", sides=()):
    m, k = a.shape
    _, n = b.shape
    tm = _pick(m, ROW_TILES)
    tn = _pick(n, COL_TILES)
    nj = n // tn
    return _call_with_side(
        functools.partial(_mm_kernel, epilogue=epilogue), "matmul_" + epilogue,
        (m // tm, nj), ("parallel", "arbitrary"),
        [pl.BlockSpec((tm, k), lambda i, j: (i, 0)), pl.BlockSpec((k, tn), lambda i, j: (0, j))],
        [a, b], pl.BlockSpec((tm, tn), lambda i, j: (i, j)), jax.ShapeDtypeStruct((m, n), out_dtype),
        sides, (m // tm) * nj, lambda i, j: i * nj + j)


def _inproj_kernel(*refs, q_tiles, k_tiles, side_arity):
    (a_ref, b_ref, cos_ref, sin_ref), (o_ref,), sides = _split_side(refs, 4, 1, side_arity)
    j = pl.program_id(1)

    def _dot():
        _run_sides(sides)
        return jnp.dot(a_ref[...], b_ref[...], preferred_element_type=F32)

    def _store_rope(scale):
        acc = _dot()
        cos = cos_ref[...]
        sin = sin_ref[...]
        if scale != 1.0:
            cos = cos * scale
            sin = sin * scale
        for h in range(acc.shape[1] // HEAD_DIM):
            cols = slice(h * HEAD_DIM, (h + 1) * HEAD_DIM)
            x = acc[:, cols]
            o_ref[:, cols] = (x * cos + pltpu.roll(x, HEAD_DIM // 2, 1) * sin).astype(o_ref.dtype)

    @pl.when(j < q_tiles)
    def _():
        _store_rope(1.0)

    @pl.when(jnp.logical_and(j >= q_tiles, j < q_tiles + k_tiles))
    def _():
        _store_rope(HEAD_DIM ** -0.5)

    @pl.when(j >= q_tiles + k_tiles)
    def _():
        o_ref[...] = _dot().astype(o_ref.dtype)


def _inproj(h, w_in, cos_t, sin_t, ret_w, seq, sides=()):
    m, k = h.shape
    _, n = w_in.shape
    tn = _pick(math.gcd(n, ret_w), COL_TILES)
    tm = _pick(seq, ROW_TILES)
    nj = n // tn
    nj_side = 1 << (nj.bit_length() - 1)
    pos_spec = pl.BlockSpec((tm, HEAD_DIM), lambda i, j: (i % (seq // tm), 0))
    return _call_with_side(
        functools.partial(_inproj_kernel, q_tiles=ret_w // tn, k_tiles=ret_w // tn), "inproj_rope",
        (m // tm, nj), ("parallel", "arbitrary"),
        [pl.BlockSpec((tm, k), lambda i, j: (i, 0)), pl.BlockSpec((k, tn), lambda i, j: (0, j)),
         pos_spec, pos_spec],
        [h, w_in, cos_t, sin_t], pl.BlockSpec((tm, tn), lambda i, j: (i, j)),
        jax.ShapeDtypeStruct((m, n), BF16),
        sides, (m // tm) * nj_side, lambda i, j: i * nj_side + jnp.minimum(j, nj_side - 1))


RET_MAX_UNROLL = 32


def _ret_kernel(decf_ref, decb_ref, q_ref, k_ref, v_ref, g_ref, o_ref, st_scr, tab_scr, *, nc):
    C = RET_CHUNK
    RET_UNROLL = math.gcd(nc, RET_MAX_UNROLL)
    lgf = -jnp.exp(jnp.broadcast_to(decf_ref[0, 0:1, :], (C, C)))
    lgb = -jnp.exp(jnp.broadcast_to(decb_ref[0, 0:1, :], (C, C)))
    row = lax.broadcasted_iota(jnp.int32, (C, C), 0).astype(F32)
    col = lax.broadcasted_iota(jnp.int32, (C, C), 1).astype(F32)
    diff = row - col
    tab_scr[0] = jnp.where(diff >= 0, jnp.exp(lgf * jnp.maximum(diff, 0.0)),
                           jnp.exp(lgb * jnp.maximum(-diff, 0.0)))
    tab_scr[1] = jnp.exp(lgf * (row + 1.0))
    tab_scr[2] = jnp.exp(lgb * (C - row))
    tab_scr[3] = jnp.exp(lgf * (C - 1.0 - row))
    tab_scr[4] = jnp.exp(lgb * row)
    cdf = jnp.exp(lgf * C)
    cdb = jnp.exp(lgb * C)

    tdims = (((0,), (0,)), ((), ()))
    ndims = (((1,), (1,)), ((), ()))

    def chunk_rows(c):
        return pl.ds(pl.multiple_of(c * C, C), C)

    def scan_body(t, carry):
        sf, sb = carry
        cf = t
        cb = nc - 1 - t
        st_scr[cf, 0:C, :] = sf.astype(BF16)
        st_scr[cb, C:2 * C, :] = sb.astype(BF16)
        rf = chunk_rows(cf)
        rb = chunk_rows(cb)
        kzf = (k_ref[rf, :].astype(F32) * tab_scr[3]).astype(BF16)
        kzb = (k_ref[rb, :].astype(F32) * tab_scr[4]).astype(BF16)
        sf = sf * cdf + lax.dot_general(kzf, v_ref[rf, :], tdims, preferred_element_type=F32)
        sb = sb * cdb + lax.dot_general(kzb, v_ref[rb, :], tdims, preferred_element_type=F32)
        return sf, sb

    zero = jnp.zeros((C, C), F32)
    lax.fori_loop(0, nc, scan_body, (zero, zero), unroll=RET_UNROLL)

    def out_body(t, carry):
        cs = [t * RET_UNROLL + i for i in range(RET_UNROLL)]
        rows = [chunk_rows(c) for c in cs]
        qs = [q_ref[r, :] for r in rows]
        ss = [lax.dot_general(q, k_ref[r, :], ndims, preferred_element_type=F32) for q, r in zip(qs, rows)]
        atts = [(s * tab_scr[0]).astype(BF16) for s in ss]
        lhss = []
        for q, att in zip(qs, atts):
            qf = q.astype(F32)
            lhss.append(jnp.concatenate([att, (qf * tab_scr[1]).astype(BF16), (qf * tab_scr[2]).astype(BF16)],
                                        axis=1))
        ys = [jnp.dot(lhs, jnp.concatenate([v_ref[r, :], st_scr[c]], axis=0), preferred_element_type=F32)
              for lhs, r, c in zip(lhss, rows, cs)]
        for y, r in zip(ys, rows):
            var = jnp.mean(y * y, axis=-1, keepdims=True)
            half_g = 0.5 * g_ref[r, :].astype(F32)
            gate = half_g + half_g * jnp.tanh(half_g)
            o_ref[r, :] = (y * lax.rsqrt(var + RMS_EPS) * gate).astype(o_ref.dtype)
        return carry

    lax.fori_loop(0, nc // RET_UNROLL, out_body, 0)


def _retention(proj, dec_f, dec_b, ret_heads, seq):
    n = proj.shape[0]
    nc = seq // RET_CHUNK
    dec_f = jnp.broadcast_to(dec_f.astype(F32).reshape(ret_heads, 1, 1), (ret_heads, 8, HEAD_DIM))
    dec_b = jnp.broadcast_to(dec_b.astype(F32).reshape(ret_heads, 1, 1), (ret_heads, 8, HEAD_DIM))
    dec_spec = pl.BlockSpec((1, 8, HEAD_DIM), lambda b, h: (h, 0, 0))

    def col_spec(base):
        return pl.BlockSpec((seq, HEAD_DIM), lambda b, h: (b, base + h))

    return pl.pallas_call(
        functools.partial(_ret_kernel, nc=nc),
        grid=(n // seq, ret_heads),
        in_specs=[dec_spec, dec_spec, col_spec(0), col_spec(ret_heads), col_spec(2 * ret_heads),
                  col_spec(3 * ret_heads)],
        out_specs=pl.BlockSpec((seq, HEAD_DIM), lambda b, h: (b, h)),
        out_shape=jax.ShapeDtypeStruct((n, ret_heads * HEAD_DIM), BF16),
        scratch_shapes=[pltpu.VMEM((nc, 2 * RET_CHUNK, RET_CHUNK), BF16),
                        pltpu.VMEM((5, RET_CHUNK, RET_CHUNK), F32)],
        compiler_params=_params(("parallel", "arbitrary")),
        name="retention",
    )(dec_f, dec_b, proj, proj, proj, proj)


SWA_MAX_QBLOCKS = 16


def _swa_kernel(sink_ref, q_ref, k_ref, v_ref, bias_ref, o_ref, *, nblk):
    B = BLOCK
    SWA_QBLOCKS = q_ref.shape[0] // B
    kvh = pl.program_id(1)
    step = pl.program_id(2)
    ndims = (((1,), (1,)), ((), ()))
    scale = HEAD_DIM ** -0.5
    exp2_coef = scale * math.log2(math.e)
    bands = []
    scores = []
    for qb in range(SWA_QBLOCKS):
        n = step * SWA_QBLOCKS + qb
        variant = jnp.where(n > 0, 0, 1) + jnp.where(n < nblk - 1, 0, 2)
        rp = pl.ds(pl.multiple_of(jnp.maximum(n - 1, 0) * B, B), B)
        rc = pl.ds(pl.multiple_of(n * B, B), B)
        rn = pl.ds(pl.multiple_of(jnp.minimum(n + 1, nblk - 1) * B, B), B)
        kband = jnp.concatenate([k_ref[rp, :], k_ref[rc, :], k_ref[rn, :]], axis=0)
        vband = jnp.concatenate([v_ref[rp, :], v_ref[rc, :], v_ref[rn, :]], axis=0)
        bands.append((vband, variant))
        for g in range(GQA_GROUP):
            q = q_ref[qb * B:(qb + 1) * B, g * HEAD_DIM:(g + 1) * HEAD_DIM]
            scores.append(lax.dot_general(q, kband, ndims, preferred_element_type=F32))
    probs = []
    for idx, s in enumerate(scores):
        qb, g = divmod(idx, GQA_GROUP)
        sink = sink_ref[kvh * GQA_GROUP + g] * (1.0 / scale)
        t = s + bias_ref[bands[qb][1], g]
        mx = jnp.maximum(jnp.max(t, axis=-1, keepdims=True), sink)
        p = jnp.exp2((t - mx) * exp2_coef)
        denom = jnp.sum(p, axis=-1, keepdims=True) + jnp.exp2((sink - mx) * exp2_coef)
        probs.append((p.astype(BF16), 1.0 / denom))
    for idx, (p, inv) in enumerate(probs):
        qb, g = divmod(idx, GQA_GROUP)
        o = jnp.dot(p, bands[qb][0], preferred_element_type=F32) * inv
        o_ref[qb * B:(qb + 1) * B, g * HEAD_DIM:(g + 1) * HEAD_DIM] = o.astype(o_ref.dtype)


def _swa(proj, sink, bias, ret_w, att_heads, seq):
    n = proj.shape[0]
    kv_heads = att_heads // GQA_GROUP
    nblk = seq // BLOCK
    qblocks = math.gcd(nblk, SWA_MAX_QBLOCKS)
    steps = nblk // qblocks
    qrows = qblocks * BLOCK
    gw = GQA_GROUP * HEAD_DIM
    q_base = 4 * ret_w // gw
    k_base = (4 * ret_w + att_heads * HEAD_DIM) // HEAD_DIM
    v_base = k_base + kv_heads
    return pl.pallas_call(
        functools.partial(_swa_kernel, nblk=nblk),
        grid=(n // seq, kv_heads, steps),
        in_specs=[pl.BlockSpec(memory_space=pltpu.SMEM),
                  pl.BlockSpec((qrows, gw), lambda b, kv, i: (b * steps + i, q_base + kv)),
                  pl.BlockSpec((seq, HEAD_DIM), lambda b, kv, i: (b, k_base + kv)),
                  pl.BlockSpec((seq, HEAD_DIM), lambda b, kv, i: (b, v_base + kv)),
                  pl.BlockSpec((4, GQA_GROUP, BLOCK, 3 * BLOCK), lambda b, kv, i: (0, kv, 0, 0))],
        out_specs=pl.BlockSpec((qrows, gw), lambda b, kv, i: (b * steps + i, kv)),
        out_shape=jax.ShapeDtypeStruct((n, att_heads * HEAD_DIM), BF16),
        compiler_params=_params(("parallel", "arbitrary", "arbitrary")),
        name="swa",
    )(sink.astype(F32), proj, proj, proj, bias)


def _t5_buckets(rel):
    nb = N_BUCKETS // 2
    max_exact = nb // 2
    base = np.where(rel > 0, nb, 0)
    n = np.abs(rel)
    large = max_exact + (np.log(np.maximum(n, 1) / max_exact) / np.log(MAX_DISTANCE / max_exact)
                         * (nb - max_exact)).astype(np.int32)
    large = np.minimum(large, nb - 1)
    return (base + np.where(n < max_exact, n, large)).astype(np.int32)


def _band_bias(rel_bias):
    qi = np.arange(BLOCK)[:, None]
    kj = np.arange(3 * BLOCK)[None, :] - BLOCK
    rel = kj - qi
    buckets = jnp.asarray(_t5_buckets(rel).reshape(-1, 1))
    onehot = (buckets == jnp.arange(N_BUCKETS, dtype=jnp.int32)[None, :]).astype(F32)
    table = jnp.dot(onehot, rel_bias.astype(F32), precision=lax.Precision.HIGHEST)
    table = jnp.transpose(table).reshape(rel_bias.shape[1], BLOCK, 3 * BLOCK) * (HEAD_DIM ** 0.5)
    in_window = np.abs(rel) <= WINDOW
    variants = []
    for v in range(4):
        ok = in_window & ((kj >= 0) | (v & 1 == 0)) & ((kj < BLOCK) | (v & 2 == 0))
        variants.append(jnp.where(jnp.asarray(ok)[None], table, NEG_INF))
    return jnp.stack(variants)


CROSS_SUBTILES = 2


def _cross_kernel(x_ref, gc_ref, gm_ref, wq_ref, kv_ref, wo_ref, x2_ref, h3_ref):
    ndims = (((1,), (1,)), ((), ()))
    mem_w = MEM_HEADS * HEAD_DIM
    tm = x_ref.shape[0]
    nsub = CROSS_SUBTILES if tm % (16 * CROSS_SUBTILES) == 0 else 1
    subs = [slice(t * (tm // nsub), (t + 1) * (tm // nsub)) for t in range(nsub)]
    head_cols = [slice(h * HEAD_DIM, (h + 1) * HEAD_DIM) for h in range(MEM_HEADS)]

    hns = []
    for rows in subs:
        x = x_ref[rows, :]
        var = jnp.mean(x * x, axis=-1, keepdims=True)
        hns.append((x * lax.rsqrt(var + RMS_EPS) * gc_ref[...]).astype(BF16))
    qs = [jnp.dot(hn, wq_ref[...], preferred_element_type=F32).astype(BF16) for hn in hns]
    scores = [[lax.dot_general(q[:, c], kv_ref[:, c], ndims, preferred_element_type=F32) for c in head_cols]
              for q in qs]
    probs = []
    for sub_scores in scores:
        sub_probs = []
        for s in sub_scores:
            s = s * (HEAD_DIM ** -0.5)
            p = jnp.exp(s - jnp.max(s, axis=-1, keepdims=True))
            sub_probs.append((p.astype(BF16), 1.0 / jnp.sum(p, axis=-1, keepdims=True)))
        probs.append(sub_probs)
    os = []
    for sub_probs in probs:
        outs = []
        for h, (p, inv) in enumerate(sub_probs):
            v = kv_ref[:, mem_w + h * HEAD_DIM:mem_w + (h + 1) * HEAD_DIM]
            outs.append((jnp.dot(p, v, preferred_element_type=F32) * inv).astype(BF16))
        os.append(jnp.concatenate(outs, axis=1))
    x2s = [jnp.dot(o, wo_ref[...], preferred_element_type=F32) + x_ref[rows, :] for o, rows in zip(os, subs)]
    for x2, rows in zip(x2s, subs):
        x2_ref[rows, :] = x2
        var2 = jnp.mean(x2 * x2, axis=-1, keepdims=True)
        h3_ref[rows, :] = (x2 * lax.rsqrt(var2 + RMS_EPS) * gm_ref[...]).astype(h3_ref.dtype)


def _cross_block(x, g_cross, g_mlp, w_cq, kvm, w_co, mem_tokens, seq):
    n, d = x.shape
    mem_w = w_cq.shape[1]
    tm = _pick(seq, (512, 256, 128, 64, 32, 16, 8))
    row_spec = pl.BlockSpec((tm, d), lambda i: (i, 0))
    gain_spec = pl.BlockSpec((1, d), lambda i: (0, 0))
    resident = pl.Buffered(1)
    return pl.pallas_call(
        _cross_kernel,
        grid=(n // tm,),
        in_specs=[row_spec, gain_spec, gain_spec,
                  pl.BlockSpec((d, mem_w), lambda i: (0, 0), pipeline_mode=resident),
                  pl.BlockSpec((mem_tokens, 2 * mem_w), lambda i: (i // (seq // tm), 0)),
                  pl.BlockSpec((mem_w, d), lambda i: (0, 0), pipeline_mode=resident)],
        out_specs=[row_spec, row_spec],
        out_shape=[jax.ShapeDtypeStruct((n, d), F32), jax.ShapeDtypeStruct((n, d), BF16)],
        compiler_params=_params(("parallel",)),
        name="cross_block",
    )(x, g_cross.reshape(1, d).astype(F32), g_mlp.reshape(1, d).astype(F32), w_cq, kvm, w_co)


def _outproj_kernel(a1_ref, a2_ref, b1_ref, b2_ref, r_ref, o_ref):
    acc = jnp.dot(a1_ref[...], b1_ref[...], preferred_element_type=F32)
    acc = acc + jnp.dot(a2_ref[...], b2_ref[...], preferred_element_type=F32)
    o_ref[...] = acc + r_ref[...]


def _outproj(y_ret, y_att, w_out, x):
    m, k1 = y_ret.shape
    _, k2 = y_att.shape
    assert k1 % k2 == 0
    n = w_out.shape[1]
    tm = _pick(m, ROW_TILES)
    tn = _pick(n, COL_TILES)
    return pl.pallas_call(
        _outproj_kernel,
        grid=(m // tm, n // tn),
        in_specs=[pl.BlockSpec((tm, k1), lambda i, j: (i, 0)),
                  pl.BlockSpec((tm, k2), lambda i, j: (i, 0)),
                  pl.BlockSpec((k1, tn), lambda i, j: (0, j)),
                  pl.BlockSpec((k2, tn), lambda i, j: (k1 // k2, j)),
                  pl.BlockSpec((tm, tn), lambda i, j: (i, j))],
        out_specs=pl.BlockSpec((tm, tn), lambda i, j: (i, j)),
        out_shape=jax.ShapeDtypeStruct((m, n), F32),
        compiler_params=_params(("parallel", "arbitrary")),
        name="outproj",
    )(y_ret, y_att, w_out, w_out, x)


def _mm_kacc_kernel(*refs, side_arity):
    (a_ref, b_ref, r_ref), (o_ref,), sides = _split_side(refs, 3, 1, side_arity)
    kk = pl.program_id(2)

    def _dot():
        _run_sides(sides)
        return jnp.dot(a_ref[...], b_ref[...], preferred_element_type=F32)

    @pl.when(kk == 0)
    def _():
        o_ref[...] = r_ref[...] + _dot()

    @pl.when(kk != 0)
    def _():
        o_ref[...] += _dot()


def _matmul_kacc(a, b, residual, sides=()):
    m, k = a.shape
    _, n = b.shape
    tm = _pick(m, ROW_TILES)
    tn = _pick(n, COL_TILES)
    tk = _pick(k, (4096, 2048) + COL_TILES)
    nj, nk = n // tn, k // tk
    return _call_with_side(
        _mm_kacc_kernel, "matmul_kacc", (m // tm, nj, nk), ("parallel", "arbitrary", "arbitrary"),
        [pl.BlockSpec((tm, tk), lambda i, j, kk: (i, kk)), pl.BlockSpec((tk, tn), lambda i, j, kk: (kk, j)),
         pl.BlockSpec((tm, tn), lambda i, j, kk: (i, j))],
        [a, b, residual], pl.BlockSpec((tm, tn), lambda i, j, kk: (i, j)),
        jax.ShapeDtypeStruct((m, n), F32),
        sides, (m // tm) * nj * nk, lambda i, j, kk: (i * nj + j) * nk + kk)


def _rope_tables(seq):
    half = HEAD_DIM // 2
    inv = ROPE_BASE ** (-jnp.arange(half, dtype=F32) / half)
    ang = jnp.arange(seq, dtype=F32)[:, None] * inv[None, :]
    cos = jnp.cos(ang)
    sin = jnp.sin(ang)
    return jnp.concatenate([cos, cos], axis=-1), jnp.concatenate([-sin, sin], axis=-1)


INPROJ_CAST = ("w_mlp_in", "w_out", "w_cq", "w_ckv")


def _layer(x, h, mem, w, l, bias, rope, seq, mem_tokens, cast, mlp_in_sides=(), mlp_out_sides=()):
    ret_heads = w["dec_f"][l].shape[0]
    att_heads = w["sink"][l].shape[0]
    ret_w = ret_heads * HEAD_DIM
    first_group = cast is None
    proj, outs = _inproj(h, w["w_in"][l], *rope, ret_w, seq,
                         sides=[_SideJob(w[name][l]) for name in INPROJ_CAST] if first_group else ())
    if first_group:
        cast = dict(zip(INPROJ_CAST, outs))
        mlp_in_sides = [_SideJob(w["w_mlp_out"][l])] + list(mlp_in_sides)
    y_ret = _retention(proj, w["dec_f"][l], w["dec_b"][l], ret_heads, seq)
    y_att = _swa(proj, w["sink"][l], bias, ret_w, att_heads, seq)
    x = _outproj(y_ret, y_att, cast["w_out"], x)

    hm = _rmsnorm(mem, w["norm_mem"][l], BF16)
    kvm, _ = _matmul(hm, cast["w_ckv"], BF16)
    x, h = _cross_block(x, w["norm_cross"][l], w["norm_mlp"][l], cast["w_cq"], kvm, w["w_co"][l],
                        mem_tokens, seq)

    a, in_outs = _matmul(h, cast["w_mlp_in"], BF16, epilogue="relu2", sides=mlp_in_sides)
    if first_group:
        cast["w_mlp_out"], in_outs = in_outs[0], in_outs[1:]
    x, out_outs = _matmul_kacc(a, cast["w_mlp_out"], x, sides=mlp_out_sides)
    return x, cast, in_outs, out_outs


def kernel(x_prompt, x_sample, mem_prompt, mem_sample, norm_mix, w_in, ret_decay_f, ret_decay_b, attn_sink,
           rel_bias, w_out, norm_cross, norm_mem, w_cq, w_ckv, w_co, norm_mlp, w_mlp_in, w_mlp_out, norm_final):
    depth = w_in.shape[0]
    w = dict(norm_mix=norm_mix, w_in=w_in.astype(BF16), dec_f=ret_decay_f, dec_b=ret_decay_b, sink=attn_sink,
             w_out=w_out, norm_cross=norm_cross, norm_mem=norm_mem, w_cq=w_cq, w_ckv=w_ckv,
             w_co=w_co.astype(BF16), norm_mlp=norm_mlp, w_mlp_in=w_mlp_in, w_mlp_out=w_mlp_out)
    bias = _band_bias(rel_bias)
    d = x_prompt.shape[-1]
    mem_tokens = mem_prompt.shape[1]
    seq_a, seq_b = x_prompt.shape[1], x_sample.shape[1]
    xa, xb = x_prompt.reshape(-1, d), x_sample.reshape(-1, d)
    ma, mb = mem_prompt.reshape(-1, d), mem_sample.reshape(-1, d)
    rope = _rope_tables(max(seq_a, seq_b))

    casts = []
    hb = None
    for l in range(depth):
        ha = _rmsnorm(xa, norm_mix[l], BF16)
        sides = [_SideJob(xb, norm_mix[0], BF16)] if l == 0 else []
        xa, cast, _, outs = _layer(xa, ha, ma, w, l, bias, rope, seq_a, mem_tokens, None, mlp_out_sides=sides)
        casts.append(cast)
        if l == 0:
            hb = outs[0]
    ya = None
    for l in range(depth):
        if l > 0:
            hb = _rmsnorm(xb, norm_mix[l], BF16)
        sides = [_SideJob(xa, norm_final, F32)] if l == 0 else []
        xb, _, outs, _ = _layer(xb, hb, mb, w, l, bias, rope, seq_b, mem_tokens, casts[l], mlp_in_sides=sides)
        if l == 0:
            ya = outs[0]
    yb = _rmsnorm(xb, norm_final, F32)
    return ya.reshape(x_prompt.shape), yb.reshape(x_sample.shape)
```
